```python
import math
import jax
import jax.numpy as jnp
from jax import lax
import numpy as np

D_MODEL = 1024
BATCH = 4
SEQ = 8192
DEPTH = 2

GRID_W = 64
CTX_LEN = 256

MLA_HEADS = 8
MLA_NOPE = 64
MLA_ROPE = 32
MLA_V = 64
MLA_Q_RANK = 384
MLA_KV_RANK = 256
MLA_WIDTH = MLA_HEADS * MLA_V
ROPE_BASE = 10000.0
Q_BLOCK = 128

SSD_HEADS = 8
SSD_HEAD_DIM = 64
SSD_WIDTH = SSD_HEADS * SSD_HEAD_DIM
SSD_GROUPS = 2
SSD_STATE = 64
SSD_CONV = 3
SSD_CHUNK = 128
SSD_CONV_DIM = SSD_WIDTH + 2 * SSD_GROUPS * SSD_STATE

MIX_SPLITS = (MLA_Q_RANK, MLA_KV_RANK, MLA_ROPE, MLA_WIDTH, SSD_WIDTH, SSD_CONV_DIM, 2 * SSD_HEADS)
MIX_IN = sum(MIX_SPLITS)
MIX_OUT = MLA_WIDTH + SSD_WIDTH

POOL_WINDOWS = (2, 4, 8, 16)
POOL_GROUPS = len(POOL_WINDOWS)
POOL_WIDTH = D_MODEL
POOL_GROUP_DIM = POOL_WIDTH // POOL_GROUPS

RMS_EPS = 1e-6

kernel_name = "hybrid_mla_ssd_pool_diffusion_block"


def _rmsnorm(x, w):
    xf = x.astype(jnp.float32)
    y = xf * lax.rsqrt(jnp.mean(xf * xf, axis=-1, keepdims=True) + RMS_EPS)
    return (y * w.astype(jnp.float32)).astype(x.dtype)


def _split(t, sizes):
    idx = np.cumsum(sizes)[:-1].tolist()
    return jnp.split(t, idx, axis=-1)


def _ident(t):
    return t


def _flip(t):
    return jnp.flip(t, axis=1)


def _axial_rope_tables(n):
    rows = n // GRID_W
    row = jnp.repeat(jnp.arange(rows, dtype=jnp.float32), GRID_W)
    col = jnp.tile(jnp.arange(GRID_W, dtype=jnp.float32), rows)
    axis_dim = MLA_ROPE // 2
    inv_freq = 1.0 / (ROPE_BASE ** (jnp.arange(0, axis_dim, 2, dtype=jnp.float32) / axis_dim))
    ang = jnp.concatenate([row[:, None] * inv_freq, col[:, None] * inv_freq], axis=-1)
    return jnp.cos(ang), jnp.sin(ang)


def _rope(t, cos, sin):
    half = t.shape[-1] // 2
    t1, t2 = t[..., :half], t[..., half:]
    out = jnp.concatenate([t1 * cos - t2 * sin, t2 * cos + t1 * sin], axis=-1)
    return out.astype(t.dtype)


def _mla_q(q_a, q_norm, w_uq):
    b, n, _ = q_a.shape
    q = (_rmsnorm(q_a, q_norm) @ w_uq).reshape(b, n, MLA_HEADS, MLA_NOPE + MLA_ROPE)
    return q[..., :MLA_NOPE], q[..., MLA_NOPE:]


def _mla_kv(kv_a, kv_norm, w_ukv):
    b, n, _ = kv_a.shape
    kv = (_rmsnorm(kv_a, kv_norm) @ w_ukv).reshape(b, n, MLA_HEADS, MLA_NOPE + MLA_V)
    return kv[..., :MLA_NOPE], kv[..., MLA_NOPE:]


def _block_attention(q_nope, q_pe, k_nope, k_pe, v):
    b, n, H, _ = q_nope.shape
    nb = n // Q_BLOCK
    scale = (MLA_NOPE + MLA_ROPE) ** -0.5

    def one_block(args):
        qn, qp = args
        s = jnp.einsum("bqhd,bkhd->bhqk", qn, k_nope) + jnp.einsum("bqhr,bkr->bhqk", qp, k_pe)
        p = jax.nn.softmax(s.astype(jnp.float32) * scale, axis=-1).astype(v.dtype)
        return jnp.einsum("bhqk,bkhd->bqhd", p, v)

    qn_b = q_nope.reshape(b, nb, Q_BLOCK, H, MLA_NOPE).transpose(1, 0, 2, 3, 4)
    qp_b = q_pe.reshape(b, nb, Q_BLOCK, H, MLA_ROPE).transpose(1, 0, 2, 3, 4)
    out = lax.map(one_block, (qn_b, qp_b))
    return out.transpose(1, 0, 2, 3, 4).reshape(b, n, H * MLA_V)


def _centred_dwconv(u, w, bias):
    n = u.shape[1]
    pad = SSD_CONV // 2
    up = jnp.pad(u, ((0, 0), (pad, pad), (0, 0)))
    out = bias
    for k in range(SSD_CONV):
        out = out + up[:, k:k + n] * w[k]
    return out


def _ssd_inputs(xbc, dt_raw, conv_w, conv_b, dt_bias):
    b, n, _ = xbc.shape
    u = jax.nn.silu(_centred_dwconv(xbc, conv_w, conv_b))
    xs, bs, cs = _split(u, (SSD_WIDTH, SSD_GROUPS * SSD_STATE, SSD_GROUPS * SSD_STATE))
    dt = jax.nn.softplus(dt_raw.astype(jnp.float32).reshape(b, n, 2, SSD_HEADS) + dt_bias.astype(jnp.float32))
    return (xs.reshape(b, n, SSD_HEADS, SSD_HEAD_DIM),
            bs.reshape(b, n, SSD_GROUPS, SSD_STATE),
            cs.reshape(b, n, SSD_GROUPS, SSD_STATE),
            dt)


def _segsum(a_cs):
    T = a_cs.shape[-1]
    diff = a_cs[..., :, None] - a_cs[..., None, :]
    return jnp.where(jnp.tril(jnp.ones((T, T), dtype=bool)), diff, -jnp.inf)


def _ssd_prepare(x, dt, A, B):
    b, L, H, P = x.shape
    G, N = B.shape[2], B.shape[3]
    R = H // G
    nc = L // SSD_CHUNK
    xd = (x.astype(jnp.float32) * dt[..., None]).reshape(b, nc, SSD_CHUNK, G, R, P)
    a = (dt * A).reshape(b, nc, SSD_CHUNK, G, R).transpose(0, 3, 4, 1, 2)
    a_cs = jnp.cumsum(a, axis=-1)
    bc = B.astype(jnp.float32).reshape(b, nc, SSD_CHUNK, G, N)
    return xd, a_cs, bc


def _ssd_pass_states(xd, a_cs, bc, h0):
    decay_to_end = jnp.exp(a_cs[..., -1:] - a_cs)
    states = jnp.einsum("bcsgn,bgrcs,bcsgrp->cbgrpn", bc, decay_to_end, xd)
    chunk_decay = jnp.exp(a_cs[..., -1]).transpose(3, 0, 1, 2)

    def step(h, inp):
        s, dec = inp
        return h * dec[..., None, None] + s, h

    return lax.scan(step, h0, (states, chunk_decay))


def _ssd_final_state(x, dt, A, B, h0):
    xd, a_cs, bc = _ssd_prepare(x, dt, A, B)
    h_final, _ = _ssd_pass_states(xd, a_cs, bc, h0)
    return h_final


def _ssd_scan(x, dt, A, B, C, h0):
    xd, a_cs, bc = _ssd_prepare(x, dt, A, B)
    h_final, h_in = _ssd_pass_states(xd, a_cs, bc, h0)
    b, nc = xd.shape[0], xd.shape[1]
    cc = C.astype(jnp.float32).reshape(b, nc, SSD_CHUNK, C.shape[2], C.shape[3])
    l_mat = jnp.exp(_segsum(a_cs))
    cb = jnp.einsum("bclgn,bcsgn->bgcls", cc, bc)
    y_diag = jnp.einsum("bgcls,bgrcls,bcsgrp->bclgrp", cb, l_mat, xd)
    y_off = jnp.einsum("bclgn,cbgrpn,bgrcl->bclgrp", cc, h_in, jnp.exp(a_cs))
    return (y_diag + y_off).reshape(x.shape), h_final


def _merge_branches(attn, gate_a, y_ssd, z, ssd_norm, w_out):
    b, n, _ = attn.shape
    y = y_ssd.reshape(b, n, SSD_WIDTH) * jax.nn.silu(z.astype(jnp.float32))
    ssd_out = _rmsnorm(y, ssd_norm).astype(attn.dtype)
    return jnp.concatenate([attn * jax.nn.silu(gate_a), ssd_out], axis=-1) @ w_out


def _mixing_sublayer(h_lat, h_ctx, need_ctx_out, cos, sin, w_in, q_norm, w_uq, kv_norm, w_ukv,
                     conv_w, conv_b, a_log, dt_bias, d_skip, ssd_norm, w_out):
    q_a_l, kv_a_l, kpe_l, ga_l, z_l, xbc_l, dtr_l = _split(h_lat @ w_in, MIX_SPLITS)
    q_a_c, kv_a_c, kpe_c, ga_c, z_c, xbc_c, dtr_c = _split(h_ctx @ w_in, MIX_SPLITS)

    kn_c, v_c = _mla_kv(kv_a_c, kv_norm, w_ukv)
    kn_l, v_l = _mla_kv(kv_a_l, kv_norm, w_ukv)
    qn_l, qp_l = _mla_q(q_a_l, q_norm, w_uq)
    qp_l = _rope(qp_l, cos[:, None, :], sin[:, None, :])
    kpe_l = _rope(kpe_l, cos, sin)
    attn_l = _block_attention(qn_l, qp_l,
                              jnp.concatenate([kn_c, kn_l], axis=1),
                              jnp.concatenate([kpe_c, kpe_l], axis=1),
                              jnp.concatenate([v_c, v_l], axis=1))

    xs_l, b_l, c_l, dt_l = _ssd_inputs(xbc_l, dtr_l, conv_w, conv_b, dt_bias)
    xs_c, b_c, c_c, dt_c = _ssd_inputs(xbc_c, dtr_c, conv_w, conv_b, dt_bias)
    bsz = h_lat.shape[0]
    h0 = jnp.zeros((bsz, SSD_GROUPS, SSD_HEADS // SSD_GROUPS, SSD_HEAD_DIM, SSD_STATE), jnp.float32)
    y_l, y_c = [], []
    for d in range(2):
        fl = _flip if d == 1 else _ident
        A = -jnp.exp(a_log[d].astype(jnp.float32))
        skip = d_skip[d].astype(jnp.float32)[:, None]
        if need_ctx_out:
            yc, hc = _ssd_scan(fl(xs_c), fl(dt_c[:, :, d]), A, fl(b_c), fl(c_c), h0)
            y_c.append(fl(yc) + skip * xs_c)
        else:
            hc = _ssd_final_state(fl(xs_c), fl(dt_c[:, :, d]), A, fl(b_c), h0)
        yl, _ = _ssd_scan(fl(xs_l), fl(dt_l[:, :, d]), A, fl(b_l), fl(c_l), hc)
        y_l.append(fl(yl) + skip * xs_l)

    out_l = _merge_branches(attn_l, ga_l, y_l[0] + y_l[1], z_l, ssd_norm, w_out)
    if not need_ctx_out:
        return out_l, None
    qn_c, qp_c = _mla_q(q_a_c, q_norm, w_uq)
    attn_c = _block_attention(qn_c, qp_c, kn_c, kpe_c, v_c)
    out_c = _merge_branches(attn_c, ga_c, y_c[0] + y_c[1], z_c, ssd_norm, w_out)
    return out_l, out_c


def _multiscale_pool(u, lin, scale):
    b, n, _ = u.shape
    uf = u.astype(jnp.float32).reshape(b, n, POOL_GROUPS, POOL_GROUP_DIM)
    csum = jnp.pad(jnp.cumsum(uf, axis=1), ((0, 0), (1, 0), (0, 0), (0, 0)))
    t = jnp.arange(n)
    diffs = []
    for g, w in enumerate(POOL_WINDOWS):
        lo = jnp.maximum(t - w // 2, 0)
        hi = jnp.minimum(t + (w - w // 2 - 1), n - 1)
        cg = csum[:, :, g]
        mean = (cg[:, hi + 1] - cg[:, lo]) / (hi - lo + 1).astype(jnp.float32)[:, None]
        diffs.append(mean - uf[:, :, g])
    m = jnp.stack(diffs, axis=2).astype(u.dtype)
    y = jnp.einsum("bngc,gcd->bngd", m, lin).reshape(b, n, POOL_WIDTH)
    return y * scale


def _pool_sublayer(h, w_in, pool_lin, pool_scale, w_out):
    u, g = _split(h @ w_in, (POOL_WIDTH, POOL_WIDTH))
    return (_multiscale_pool(u, pool_lin, pool_scale) * jax.nn.silu(g)) @ w_out


def setup_inputs(seed: int = 0) -> dict:
    key = jax.random.key(seed)
    keys = list(jax.random.split(key, 32))
    f32 = jnp.float32
    ne, no = (DEPTH + 1) // 2, DEPTH // 2

    def nrm(k, shape, scale):
        return jax.random.normal(k, shape, f32) * scale

    def gain(k, shape):
        return 1.0 + 0.02 * jax.random.normal(k, shape, f32)

    dt0 = jnp.exp(jax.random.uniform(keys[14], (ne, 2, SSD_HEADS), f32, math.log(1e-3), math.log(1e-1)))
    return {
        "x": nrm(keys[0], (BATCH, SEQ, D_MODEL), 1.0),
        "c": nrm(keys[1], (BATCH, D_MODEL), 1.0),
        "ctx": nrm(keys[2], (BATCH, CTX_LEN, D_MODEL), 1.0),
        "c_ctx": nrm(keys[3], (D_MODEL,), 1.0),
        "mod_w": nrm(keys[4], (DEPTH, D_MODEL, 3 * D_MODEL), D_MODEL ** -0.5),
        "mod_b": nrm(keys[5], (DEPTH, 3 * D_MODEL), 0.02),
        "norm_w": gain(keys[6], (DEPTH, D_MODEL)),
        "w_in_mix": nrm(keys[7], (ne, D_MODEL, MIX_IN), D_MODEL ** -0.5),
        "q_norm": gain(keys[8], (ne, MLA_Q_RANK)),
        "w_uq": nrm(keys[9], (ne, MLA_Q_RANK, MLA_HEADS * (MLA_NOPE + MLA_ROPE)), MLA_Q_RANK ** -0.5),
        "kv_norm": gain(keys[10], (ne, MLA_KV_RANK)),
        "w_ukv": nrm(keys[11], (ne, MLA_KV_RANK, MLA_HEADS * (MLA_NOPE + MLA_V)), MLA_KV_RANK ** -0.5),
        "conv_w": nrm(keys[12], (ne, SSD_CONV, SSD_CONV_DIM), SSD_CONV ** -0.5),
        "conv_b": nrm(keys[13], (ne, SSD_CONV_DIM), 0.02),
        "a_log": jnp.log(jax.random.uniform(keys[15], (ne, 2, SSD_HEADS), f32, 1.0, 16.0)),
        "dt_bias": dt0 + jnp.log(-jnp.expm1(-dt0)),
        "d_skip": 1.0 + 0.1 * jax.random.normal(keys[16], (ne, 2, SSD_HEADS), f32),
        "ssd_norm": gain(keys[17], (ne, SSD_WIDTH)),
        "w_out_mix": nrm(keys[18], (ne, MIX_OUT, D_MODEL), MIX_OUT ** -0.5),
        "w_in_pool": nrm(keys[19], (no, D_MODEL, 2 * POOL_WIDTH), D_MODEL ** -0.5),
        "pool_lin": nrm(keys[20], (no, POOL_GROUPS, POOL_GROUP_DIM, POOL_GROUP_DIM), POOL_GROUP_DIM ** -0.5),
        "pool_scale": 1.0 + 0.1 * jax.random.normal(keys[21], (no, POOL_WIDTH), f32),
        "w_out_pool": nrm(keys[22], (no, POOL_WIDTH, D_MODEL), POOL_WIDTH ** -0.5),
        "final_norm": gain(keys[23], (D_MODEL,)),
    }


def reference(x, c, ctx, c_ctx, mod_w, mod_b, norm_w, w_in_mix, q_norm, w_uq, kv_norm, w_ukv,
              conv_w, conv_b, a_log, dt_bias, d_skip, ssd_norm, w_out_mix, w_in_pool, pool_lin,
              pool_scale, w_out_pool, final_norm):
    n = x.shape[1]
    cos, sin = _axial_rope_tables(n)
    last_mix = ((DEPTH - 1) // 2) * 2
    silu_c = jax.nn.silu(c)
    silu_cc = jax.nn.silu(c_ctx)
    for i in range(DEPTH):
        j = i // 2
        need_ctx_out = i < last_mix
        shift, scale, gate = jnp.split((silu_c @ mod_w[i] + mod_b[i])[:, None, :], 3, axis=-1)
        h = _rmsnorm(x, norm_w[i]) * (1 + scale) + shift
        if i % 2 == 0 or need_ctx_out:
            shift_c, scale_c, gate_c = jnp.split(silu_cc @ mod_w[i] + mod_b[i], 3, axis=-1)
            h_c = _rmsnorm(ctx, norm_w[i]) * (1 + scale_c) + shift_c
        if i % 2 == 0:
            o, o_c = _mixing_sublayer(h, h_c, need_ctx_out, cos, sin, w_in_mix[j], q_norm[j], w_uq[j],
                                      kv_norm[j], w_ukv[j], conv_w[j], conv_b[j], a_log[j], dt_bias[j],
                                      d_skip[j], ssd_norm[j], w_out_mix[j])
        else:
            o = _pool_sublayer(h, w_in_pool[j], pool_lin[j], pool_scale[j], w_out_pool[j])
            o_c = _pool_sublayer(h_c, w_in_pool[j], pool_lin[j], pool_scale[j], w_out_pool[j]) if need_ctx_out else None
        x = x + gate * o
        if need_ctx_out:
            ctx = ctx + gate_c * o_c
    return _rmsnorm(x, final_norm)
```

```python
import functools
import math

import jax
import jax.numpy as jnp
import numpy as np
from jax import lax
from jax.experimental import pallas as pl
from jax.experimental.pallas import tpu as pltpu

D_MODEL = 1024
GRID_W = 64

MLA_HEADS = 8
MLA_NOPE = 64
MLA_ROPE = 32
MLA_V = 64
MLA_Q_RANK = 384
MLA_KV_RANK = 256
MLA_WIDTH = MLA_HEADS * MLA_V
ROPE_BASE = 10000.0
HEAD_PAD = 128
V_ROWS = 80

SSD_HEADS = 8
SSD_HEAD_DIM = 64
SSD_WIDTH = SSD_HEADS * SSD_HEAD_DIM
SSD_GROUPS = 2
SSD_STATE = 64
SSD_CONV = 3
SSD_CHUNK = 128
SSD_CONV_DIM = SSD_WIDTH + 2 * SSD_GROUPS * SSD_STATE
MIX_SPLITS = (MLA_Q_RANK, MLA_KV_RANK, MLA_ROPE, MLA_WIDTH, SSD_WIDTH, SSD_CONV_DIM, 2 * SSD_HEADS)

POOL_WINDOWS = (2, 4, 8, 16)
POOL_GROUPS = len(POOL_WINDOWS)
POOL_WIDTH = D_MODEL
POOL_GROUP_DIM = POOL_WIDTH // POOL_GROUPS
POOL_HALO = 8

RMS_EPS = 1e-6
LANES = 128
SUBLANES = 8
VMEM_LIMIT = 56 * 1024 * 1024

C_QA = 0
C_KVA = C_QA + MLA_Q_RANK
C_KPE = C_KVA + MLA_KV_RANK
C_KPS = C_KPE + LANES
C_Z = C_KPS + LANES
C_XBC = C_Z + SSD_WIDTH
C_DT = C_XBC + SSD_CONV_DIM
C_END = C_DT + LANES

BF16 = jnp.bfloat16
F32 = jnp.float32
NEG_BIG = -1e30


def _rms(x, w):
    y = x * lax.rsqrt(jnp.mean(x * x, axis=-1, keepdims=True) + RMS_EPS)
    return y * w


def _silu(x):
    return x * (1.0 / (1.0 + jnp.exp(-x)))


def _softplus(x):
    return jnp.maximum(x, 0.0) + jnp.log(1.0 + jnp.exp(-jnp.abs(x)))


def _dot(a, b):
    return jnp.dot(a, b, preferred_element_type=F32)


def _dot_nt(a, b):
    return lax.dot_general(a, b, (((1,), (1,)), ((), ())), preferred_element_type=F32)


def _dot_tn(a, b):
    return lax.dot_general(a, b, (((0,), (0,)), ((), ())), preferred_element_type=F32)


def _params(sem):
    return pltpu.CompilerParams(dimension_semantics=sem, vmem_limit_bytes=VMEM_LIMIT)


def _mod_kernel(c_ref, w_ref, b_ref, o_ref):
    sc = _silu(c_ref[...]).astype(BF16)
    o_ref[0] = _dot(sc, w_ref[0].astype(BF16)) + b_ref[0]


def _modulation(cvec, mod_w, mod_b):
    depth, d, d3 = mod_w.shape
    rows = cvec.shape[0]
    nblk = d3 // d
    return pl.pallas_call(
        _mod_kernel,
        out_shape=jax.ShapeDtypeStruct((depth, rows, d3), F32),
        grid=(depth, nblk),
        in_specs=[
            pl.BlockSpec((rows, d), lambda i, j: (0, 0)),
            pl.BlockSpec((1, d, d), lambda i, j: (i, 0, j)),
            pl.BlockSpec((1, 1, d), lambda i, j: (i, 0, j)),
        ],
        out_specs=pl.BlockSpec((1, rows, d), lambda i, j: (i, 0, j)),
        compiler_params=_params(("arbitrary", "arbitrary")),
        name="modulation",
    )(cvec, mod_w, mod_b.reshape(depth, 1, d3))


def _mix_in_kernel(nct, nbatch, x_ref, ctx_ref, mod_ref, nw_ref, wn_ref, wg_ref, qn_ref, kvn_ref,
                   wuq_ref, wuk_ref, wuv_ref, cosq_ref, sinq_ref, cosk_ref, sink_ref,
                   q_ref, k_ref, v_ref, g_ref, sz_ref, xbc_ref, dt_ref):
    b = pl.program_id(0)
    t = pl.program_id(1)
    is_ctx = t < nct
    xin = jnp.where(is_ctx, ctx_ref[0], x_ref[0])
    row = jnp.where(is_ctx, nbatch, b)
    mod = mod_ref[pl.ds(row, 1), :]
    shift = mod[:, 0:D_MODEL]
    scale = mod[:, D_MODEL:2 * D_MODEL]
    h = _rms(xin, nw_ref[...]) * (1.0 + scale) + shift
    hb = h.astype(BF16)
    proj = _dot(hb, wn_ref[...])

    xbc_ref[0] = proj[:, C_XBC:C_DT]
    dt_ref[0] = proj[:, C_DT:C_END]

    ckv = _rms(proj[:, C_KVA:C_KPE], kvn_ref[...]).astype(BF16)
    kpad = _dot(ckv, wuk_ref[...])
    kpe = proj[:, C_KPE:C_KPS] * cosk_ref[...] + proj[:, C_KPS:C_Z] * sink_ref[...]
    vt = _dot_nt(wuv_ref[...], ckv)
    tm = vt.shape[1]
    ones_rows = (lax.broadcasted_iota(jnp.int32, (V_ROWS - MLA_V, tm), 0) == 0).astype(BF16)
    for hd in range(MLA_HEADS):
        k_ref[0, hd] = (kpad[:, hd * HEAD_PAD:(hd + 1) * HEAD_PAD] + kpe).astype(BF16)
        v_ref[0, hd, 0, 0:MLA_V, :] = vt[hd * MLA_V:(hd + 1) * MLA_V].astype(BF16)
        v_ref[0, hd, 0, MLA_V:V_ROWS, :] = ones_rows

    @pl.when(jnp.logical_not(is_ctx))
    def _():
        sz_ref[0] = _silu(proj[:, C_Z:C_XBC])
        g_ref[0] = _silu(_dot_nt(wg_ref[...], hb))
        cq = _rms(proj[:, C_QA:C_KVA], qn_ref[...]).astype(BF16)
        qt = _dot_nt(wuq_ref[...], cq)
        sm_scale = (MLA_NOPE + MLA_ROPE) ** -0.5
        cos = cosq_ref[...]
        sin = sinq_ref[...]
        half = MLA_ROPE // 2
        for hd in range(MLA_HEADS):
            base = hd * HEAD_PAD
            t1 = qt[base + MLA_NOPE:base + MLA_NOPE + half]
            t2 = qt[base + MLA_NOPE + half:base + MLA_NOPE + MLA_ROPE]
            q_ref[0, hd, 0:MLA_NOPE, :] = (qt[base:base + MLA_NOPE] * sm_scale).astype(BF16)
            q_ref[0, hd, MLA_NOPE:MLA_NOPE + half, :] = ((t1 * cos - t2 * sin) * sm_scale).astype(BF16)
            q_ref[0, hd, MLA_NOPE + half:MLA_NOPE + MLA_ROPE, :] = ((t2 * cos + t1 * sin) * sm_scale).astype(BF16)
            q_ref[0, hd, MLA_NOPE + MLA_ROPE:HEAD_PAD, :] = jnp.zeros((HEAD_PAD - MLA_NOPE - MLA_ROPE, tm), BF16)


def _mix_in(x, ctx, mod0, norm_w, wn, wg_t, q_norm, kv_norm, wuq_t, wuk_p, wuv_t,
            cos_q, sin_q, cos_k, sin_k, tm):
    bsz, n, d = x.shape
    nctx = ctx.shape[1]
    nct = nctx // tm
    nlt = n // tm
    tt = nctx + n
    rows = mod0.shape[0]

    def lat(t):
        return jnp.maximum(t - nct, 0)

    const = lambda b, t: (0, 0)
    out_shape = (
        jax.ShapeDtypeStruct((bsz, MLA_HEADS, HEAD_PAD, n), BF16),
        jax.ShapeDtypeStruct((bsz, MLA_HEADS, tt, HEAD_PAD), BF16),
        jax.ShapeDtypeStruct((bsz, MLA_HEADS, tt // tm, V_ROWS, tm), BF16),
        jax.ShapeDtypeStruct((bsz, MLA_WIDTH, n), F32),
        jax.ShapeDtypeStruct((bsz, n, SSD_WIDTH), F32),
        jax.ShapeDtypeStruct((bsz, tt, SSD_CONV_DIM), F32),
        jax.ShapeDtypeStruct((bsz, tt, LANES), F32),
    )
    out_specs = (
        pl.BlockSpec((1, MLA_HEADS, HEAD_PAD, tm), lambda b, t: (b, 0, 0, lat(t))),
        pl.BlockSpec((1, MLA_HEADS, tm, HEAD_PAD), lambda b, t: (b, 0, t, 0)),
        pl.BlockSpec((1, MLA_HEADS, 1, V_ROWS, tm), lambda b, t: (b, 0, t, 0, 0)),
        pl.BlockSpec((1, MLA_WIDTH, tm), lambda b, t: (b, 0, lat(t))),
        pl.BlockSpec((1, tm, SSD_WIDTH), lambda b, t: (b, lat(t), 0)),
        pl.BlockSpec((1, tm, SSD_CONV_DIM), lambda b, t: (b, t, 0)),
        pl.BlockSpec((1, tm, LANES), lambda b, t: (b, t, 0)),
    )
    in_specs = [
        pl.BlockSpec((1, tm, d), lambda b, t: (b, lat(t), 0)),
        pl.BlockSpec((1, tm, d), lambda b, t: (b, jnp.minimum(t, nct - 1), 0)),
        pl.BlockSpec((rows, 3 * d), const),
        pl.BlockSpec((1, d), const),
        pl.BlockSpec(wn.shape, const),
        pl.BlockSpec(wg_t.shape, const),
        pl.BlockSpec((1, MLA_Q_RANK), const),
        pl.BlockSpec((1, MLA_KV_RANK), const),
        pl.BlockSpec(wuq_t.shape, const),
        pl.BlockSpec(wuk_p.shape, const),
        pl.BlockSpec(wuv_t.shape, const),
        pl.BlockSpec((MLA_ROPE // 2, tm), lambda b, t: (0, lat(t))),
        pl.BlockSpec((MLA_ROPE // 2, tm), lambda b, t: (0, lat(t))),
        pl.BlockSpec((tm, LANES), lambda b, t: (t, 0)),
        pl.BlockSpec((tm, LANES), lambda b, t: (t, 0)),
    ]
    return pl.pallas_call(
        functools.partial(_mix_in_kernel, nct, bsz),
        out_shape=out_shape,
        grid=(bsz, nct + nlt),
        in_specs=in_specs,
        out_specs=out_specs,
        compiler_params=_params(("arbitrary", "arbitrary")),
        name="mix_in",
    )(x, ctx, mod0, norm_w, wn, wg_t, q_norm, kv_norm, wuq_t, wuk_p, wuv_t, cos_q, sin_q, cos_k, sin_k)


def _attn_kernel(nkb, sub, kt, q_ref, k_ref, v_ref, g_ref, o_ref):
    q = q_ref[0, 0]
    tq = q.shape[1]
    kb = sub * kt

    def body(j, carry):
        m, acc = carry
        ks = k_ref[0, 0, pl.ds(pl.multiple_of(j * kb, kb), kb), :]
        s = _dot(ks, q)
        mn = jnp.maximum(m, jnp.max(s, axis=0, keepdims=True))
        alpha = jnp.exp(m - mn)
        p = jnp.exp(s - mn).astype(BF16)
        pv = _dot(v_ref[0, 0, j * sub], p[0:kt])
        for i in range(1, sub):
            pv = pv + _dot(v_ref[0, 0, j * sub + i], p[i * kt:(i + 1) * kt])
        return mn, acc * alpha + pv

    m0 = jnp.full((1, tq), NEG_BIG, F32)
    acc0 = jnp.zeros((V_ROWS, tq), F32)
    _, acc = lax.fori_loop(0, nkb, body, (m0, acc0))
    out = acc[0:MLA_V] / acc[MLA_V:MLA_V + 1]
    o_ref[0] = (out * g_ref[0]).astype(BF16)


def _attention(q_t, kcat, v_t, g_t, tq):
    bsz, nh, _, n = q_t.shape
    tt = kcat.shape[2]
    nvb, kt = v_t.shape[2], v_t.shape[4]
    sub = 3 if nvb % 3 == 0 else 1
    nkb = nvb // sub
    return pl.pallas_call(
        functools.partial(_attn_kernel, nkb, sub, kt),
        out_shape=jax.ShapeDtypeStruct((bsz, nh * MLA_V, n), BF16),
        grid=(bsz, nh, n // tq),
        in_specs=[
            pl.BlockSpec((1, 1, HEAD_PAD, tq), lambda b, h, i: (b, h, 0, i)),
            pl.BlockSpec((1, 1, tt, HEAD_PAD), lambda b, h, i: (b, h, 0, 0)),
            pl.BlockSpec((1, 1, nvb, V_ROWS, kt), lambda b, h, i: (b, h, 0, 0, 0)),
            pl.BlockSpec((1, MLA_V, tq), lambda b, h, i: (b, h, i)),
        ],
        out_specs=pl.BlockSpec((1, MLA_V, tq), lambda b, h, i: (b, h, i)),
        compiler_params=_params(("arbitrary", "arbitrary", "arbitrary")),
        name="attention",
    )(q_t, kcat, v_t, g_t)


def _ssd_direction(d, first_seg, last_seg, xc_ref, xp_ref, xn_ref, dt_ref, cw_ref, cb_ref, alog_ref,
                   dtb_ref, skip_ref, state_ref, y_ref, write_y):
    L = SSD_CHUNK
    xc = xc_ref[0]
    ridx = lax.broadcasted_iota(jnp.int32, (L, 1), 0)
    prev_row = jnp.where(first_seg, 0.0, xp_ref[0, SUBLANES - 1:SUBLANES, :])
    next_row = jnp.where(last_seg, 0.0, xn_ref[0, 0:1, :])
    x_m1 = jnp.where(ridx == 0, prev_row, pltpu.roll(xc, 1, axis=0))
    x_p1 = jnp.where(ridx == L - 1, next_row, pltpu.roll(xc, L - 1, axis=0))
    cw = cw_ref[...]
    conv = cb_ref[...] + x_m1 * cw[0:1] + xc * cw[1:2] + x_p1 * cw[2:3]
    u = _silu(conv)
    xs = u[:, 0:SSD_WIDTH]
    gn = SSD_GROUPS * SSD_STATE
    bm = u[:, SSD_WIDTH:SSD_WIDTH + gn].astype(BF16)
    cm = u[:, SSD_WIDTH + gn:SSD_WIDTH + 2 * gn].astype(BF16)

    dtm = _softplus(dt_ref[0] + dtb_ref[...])
    a = dtm * (-jnp.exp(alog_ref[...]))
    li = lax.broadcasted_iota(jnp.int32, (L, L), 0)
    si = lax.broadcasted_iota(jnp.int32, (L, L), 1)
    mask = (si <= li) if d == 0 else (si >= li)
    acs = jnp.dot(mask.astype(F32), a, preferred_element_type=F32, precision=lax.Precision.HIGHEST)
    acs_t = acs.T
    end = L - 1 if d == 0 else 0
    skip = skip_ref[...]

    heads_per_group = SSD_HEADS // SSD_GROUPS
    for g in range(SSD_GROUPS):
        bg = bm[:, g * SSD_STATE:(g + 1) * SSD_STATE]
        cg = cm[:, g * SSD_STATE:(g + 1) * SSD_STATE]
        cbm = _dot_nt(cg, bg)
        for r in range(heads_per_group):
            hd = g * heads_per_group + r
            col = d * SSD_HEADS + hd
            a_col = acs[:, col:col + 1]
            a_row = acs_t[col:col + 1, :]
            a_end = acs[end:end + 1, col:col + 1]
            xs_h = xs[:, hd * SSD_HEAD_DIM:(hd + 1) * SSD_HEAD_DIM]
            xd = xs_h * dtm[:, col:col + 1]
            h_in = state_ref[d, hd]
            if write_y is not None:
                lmat = jnp.exp(jnp.where(mask, a_col - a_row, -jnp.inf))
                y = _dot((cbm * lmat).astype(BF16), xd.astype(BF16))
                y = y + jnp.exp(a_col) * _dot_nt(cg, h_in.astype(BF16))
                y = y + skip[:, col:col + 1] * xs_h

                @pl.when(write_y)
                def _():
                    y_ref[0, :, hd * SSD_HEAD_DIM:(hd + 1) * SSD_HEAD_DIM] = y
            xdw = (xd * jnp.exp(a_end - a_col)).astype(BF16)
            state_ref[d, hd] = h_in * jnp.exp(a_end) + _dot_tn(xdw, bg)


def _ssd_kernel(ncc, nchunks, xf_ref, xfp_ref, xfn_ref, dtf_ref, xb_ref, xbp_ref, xbn_ref, dtb_raw_ref,
                cw_ref, cb_ref, alog_ref, dtbias_ref, skip_ref, yf_ref, yb_ref, state_ref):
    s = pl.program_id(1)

    @pl.when(s == 0)
    def _():
        state_ref[...] = jnp.zeros_like(state_ref)

    cf = s
    cbk = jnp.where(s < ncc, ncc - 1 - s, nchunks - 1 - (s - ncc))
    first = lambda c: jnp.logical_or(c == 0, c == ncc)
    last = lambda c: jnp.logical_or(c == ncc - 1, c == nchunks - 1)
    is_lat = s >= ncc
    _ssd_direction(0, first(cf), last(cf), xf_ref, xfp_ref, xfn_ref, dtf_ref, cw_ref, cb_ref, alog_ref,
                   dtbias_ref, skip_ref, state_ref, yf_ref, is_lat)
    _ssd_direction(1, first(cbk), last(cbk), xb_ref, xbp_ref, xbn_ref, dtb_raw_ref, cw_ref, cb_ref, alog_ref,
                   dtbias_ref, skip_ref, state_ref, yb_ref, is_lat)


def _ssd(xbc, dt_raw, conv_w, conv_b, alog_row, dtbias_row, skip_row, nctx):
    bsz, tt, cd = xbc.shape
    L = SSD_CHUNK
    nchunks = tt // L
    ncc = nctx // L
    n = tt - nctx
    r8 = L // SUBLANES
    nrow8 = tt // SUBLANES

    def cf(s):
        return s

    def cb(s):
        return jnp.where(s < ncc, ncc - 1 - s, nchunks - 1 - (s - ncc))

    def specs(cfn):
        return [
            pl.BlockSpec((1, L, cd), lambda b, s: (b, cfn(s), 0)),
            pl.BlockSpec((1, SUBLANES, cd), lambda b, s: (b, jnp.maximum(cfn(s) * r8 - 1, 0), 0)),
            pl.BlockSpec((1, SUBLANES, cd), lambda b, s: (b, jnp.minimum((cfn(s) + 1) * r8, nrow8 - 1), 0)),
            pl.BlockSpec((1, L, LANES), lambda b, s: (b, cfn(s), 0)),
        ]

    const = lambda b, s: (0, 0)
    in_specs = specs(cf) + specs(cb) + [
        pl.BlockSpec((SSD_CONV, cd), const),
        pl.BlockSpec((1, cd), const),
        pl.BlockSpec((1, LANES), const),
        pl.BlockSpec((1, LANES), const),
        pl.BlockSpec((1, LANES), const),
    ]
    out_specs = (
        pl.BlockSpec((1, L, SSD_WIDTH), lambda b, s: (b, jnp.maximum(cf(s) - ncc, 0), 0)),
        pl.BlockSpec((1, L, SSD_WIDTH), lambda b, s: (b, nchunks - 1 - ncc - jnp.maximum(s - ncc, 0), 0)),
    )
    y_shape = jax.ShapeDtypeStruct((bsz, n, SSD_WIDTH), F32)
    return pl.pallas_call(
        functools.partial(_ssd_kernel, ncc, nchunks),
        out_shape=(y_shape, y_shape),
        grid=(bsz, nchunks),
        in_specs=in_specs,
        out_specs=out_specs,
        scratch_shapes=[pltpu.VMEM((2, SSD_HEADS, SSD_HEAD_DIM, SSD_STATE), F32)],
        compiler_params=_params(("arbitrary", "arbitrary")),
        name="ssd",
    )(xbc, xbc, xbc, dt_raw, xbc, xbc, xbc, dt_raw, conv_w, conv_b, alog_row, dtbias_row, skip_row)


def _mix_out_kernel(nbatch, a_ref, yf_ref, yb_ref, sz_ref, x_ref, mod0_ref, mod1_ref, sn_ref, woa_ref, wos_ref,
                    nw1_ref, wp_ref, x1_ref, u_ref, sg_ref):
    b = pl.program_id(0)
    y = (yf_ref[0] + yb_ref[0]) * sz_ref[0]
    ssd = _rms(y, sn_ref[...]).astype(BF16)
    o = _dot_tn(a_ref[0], woa_ref[...]) + _dot(ssd, wos_ref[...])
    gate0 = mod0_ref[pl.ds(b, 1), 2 * D_MODEL:3 * D_MODEL]
    x1 = x_ref[0] + gate0 * o
    x1_ref[0] = x1
    mod1 = mod1_ref[pl.ds(b, 1), :]
    h1 = _rms(x1, nw1_ref[...]) * (1.0 + mod1[:, D_MODEL:2 * D_MODEL]) + mod1[:, 0:D_MODEL]
    ug = _dot(h1.astype(BF16), wp_ref[...])
    u_ref[0] = ug[:, 0:POOL_WIDTH]
    sg_ref[0] = _silu(ug[:, POOL_WIDTH:2 * POOL_WIDTH])


def _mix_out(a_t, yf, yb, sz, x, mod0, mod1, ssd_norm, wo_a, wo_s, norm_w1, w_pool, tm):
    bsz, n, d = x.shape
    rows = mod0.shape[0]
    const = lambda b, t: (0, 0)
    tile = lambda w: pl.BlockSpec((1, tm, w), lambda b, t: (b, t, 0))
    return pl.pallas_call(
        functools.partial(_mix_out_kernel, bsz),
        out_shape=(jax.ShapeDtypeStruct((bsz, n, d), F32),
                   jax.ShapeDtypeStruct((bsz, n, POOL_WIDTH), F32),
                   jax.ShapeDtypeStruct((bsz, n, POOL_WIDTH), F32)),
        grid=(bsz, n // tm),
        in_specs=[
            pl.BlockSpec((1, MLA_WIDTH, tm), lambda b, t: (b, 0, t)),
            tile(SSD_WIDTH), tile(SSD_WIDTH), tile(SSD_WIDTH), tile(d),
            pl.BlockSpec((rows, 3 * d), const),
            pl.BlockSpec((rows, 3 * d), const),
            pl.BlockSpec((1, SSD_WIDTH), const),
            pl.BlockSpec(wo_a.shape, const),
            pl.BlockSpec(wo_s.shape, const),
            pl.BlockSpec((1, d), const),
            pl.BlockSpec(w_pool.shape, const),
        ],
        out_specs=(tile(d), tile(POOL_WIDTH), tile(POOL_WIDTH)),
        compiler_params=_params(("arbitrary", "arbitrary")),
        name="mix_out",
    )(a_t, yf, yb, sz, x, mod0, mod1, ssd_norm, wo_a, wo_s, norm_w1, w_pool)


def _pool_kernel(n, u_ref, up_ref, un_ref, sg_ref, x1_ref, mod1_ref, lin_ref, ps_ref, wo_ref, fn_ref,
                 o_ref, ext_ref):
    b = pl.program_id(0)
    t = pl.program_id(1)
    nt = pl.num_programs(1)
    tm = u_ref.shape[1]
    H = POOL_HALO
    ext_ref[0:H, :] = jnp.where(t == 0, 0.0, up_ref[0])
    ext_ref[H:H + tm, :] = u_ref[0]
    ext_ref[H + tm:H + tm + H, :] = jnp.where(t == nt - 1, 0.0, un_ref[0])

    tok = t * tm + lax.broadcasted_iota(jnp.int32, (tm, 1), 0)
    ys = []
    for g, w in enumerate(POOL_WINDOWS):
        lanes = slice(g * POOL_GROUP_DIM, (g + 1) * POOL_GROUP_DIM)
        acc = ext_ref[H - w // 2:H - w // 2 + tm, lanes]
        for j in range(1, w):
            acc = acc + ext_ref[H - w // 2 + j:H - w // 2 + j + tm, lanes]
        lo = jnp.maximum(tok - w // 2, 0)
        hi = jnp.minimum(tok + (w - w // 2 - 1), n - 1)
        cnt = (hi - lo + 1).astype(F32)
        m = acc / cnt - u_ref[0, :, lanes]
        ys.append(_dot(m.astype(BF16), lin_ref[g]))
    y = jnp.concatenate(ys, axis=-1) * ps_ref[...] * sg_ref[0]
    o = _dot(y.astype(BF16), wo_ref[...])
    gate1 = mod1_ref[pl.ds(b, 1), 2 * D_MODEL:3 * D_MODEL]
    x2 = x1_ref[0] + gate1 * o
    o_ref[0] = _rms(x2, fn_ref[...])


def _pool_out(u, sg, x1, mod1, pool_lin, pool_scale, wo_pool, final_norm, tm):
    bsz, n, d = x1.shape
    rows = mod1.shape[0]
    r8 = tm // SUBLANES
    nrow8 = n // SUBLANES
    const = lambda b, t: (0, 0)
    tile = lambda w: pl.BlockSpec((1, tm, w), lambda b, t: (b, t, 0))
    return pl.pallas_call(
        functools.partial(_pool_kernel, n),
        out_shape=jax.ShapeDtypeStruct((bsz, n, d), F32),
        grid=(bsz, n // tm),
        in_specs=[
            tile(POOL_WIDTH),
            pl.BlockSpec((1, POOL_HALO, POOL_WIDTH), lambda b, t: (b, jnp.maximum(t * r8 - 1, 0), 0)),
            pl.BlockSpec((1, POOL_HALO, POOL_WIDTH), lambda b, t: (b, jnp.minimum((t + 1) * r8, nrow8 - 1), 0)),
            tile(POOL_WIDTH), tile(d),
            pl.BlockSpec((rows, 3 * d), const),
            pl.BlockSpec(pool_lin.shape, lambda b, t: (0, 0, 0)),
            pl.BlockSpec((1, POOL_WIDTH), const),
            pl.BlockSpec(wo_pool.shape, const),
            pl.BlockSpec((1, d), const),
        ],
        out_specs=tile(d),
        scratch_shapes=[pltpu.VMEM((tm + 2 * POOL_HALO, POOL_WIDTH), F32)],
        compiler_params=_params(("arbitrary", "arbitrary")),
        name="pool_out",
    )(u, u, u, sg, x1, mod1, pool_lin, pool_scale, wo_pool, final_norm)


def _rope_tables(n, nctx):
    rows = n // GRID_W
    row = jnp.repeat(jnp.arange(rows, dtype=F32), GRID_W)
    col = jnp.tile(jnp.arange(GRID_W, dtype=F32), rows)
    axis_dim = MLA_ROPE // 2
    inv_freq = 1.0 / (ROPE_BASE ** (jnp.arange(0, axis_dim, 2, dtype=F32) / axis_dim))
    ang = jnp.concatenate([row[:, None] * inv_freq, col[:, None] * inv_freq], axis=-1)
    cos, sin = jnp.cos(ang), jnp.sin(ang)
    half = MLA_ROPE // 2
    cos_k = jnp.zeros((nctx + n, LANES), F32)
    sin_k = jnp.zeros((nctx + n, LANES), F32)
    cos_all = jnp.concatenate([jnp.ones((nctx, half), F32), cos], axis=0)
    sin_all = jnp.concatenate([jnp.zeros((nctx, half), F32), sin], axis=0)
    cos_k = cos_k.at[:, MLA_NOPE:MLA_NOPE + half].set(cos_all).at[:, MLA_NOPE + half:MLA_NOPE + MLA_ROPE].set(cos_all)
    sin_k = sin_k.at[:, MLA_NOPE:MLA_NOPE + half].set(-sin_all).at[:, MLA_NOPE + half:MLA_NOPE + MLA_ROPE].set(sin_all)
    return cos.T, sin.T, cos_k, sin_k


def _pad_row(v):
    v = v.reshape(1, -1).astype(F32)
    return jnp.pad(v, ((0, 0), (0, LANES - v.shape[1])))


def kernel(x, c, ctx, c_ctx, mod_w, mod_b, norm_w, w_in_mix, q_norm, w_uq, kv_norm, w_ukv, conv_w, conv_b,
           a_log, dt_bias, d_skip, ssd_norm, w_out_mix, w_in_pool, pool_lin, pool_scale, w_out_pool, final_norm):
    bsz, n, d = x.shape
    nctx = ctx.shape[1]
    tm = 256
    assert d == D_MODEL and n % tm == 0 and nctx % tm == 0 and n % GRID_W == 0
    assert mod_w.shape[0] == 2, "two layers: one mixing layer followed by one pooling layer"

    rows = -(-(bsz + 1) // SUBLANES) * SUBLANES
    cvec = jnp.zeros((rows, d), F32).at[:bsz].set(c).at[bsz].set(c_ctx)
    mods = _modulation(cvec, mod_w, mod_b)
    mod0, mod1 = mods[0], mods[1]

    w = w_in_mix[0]
    offs = np.cumsum((0,) + MIX_SPLITS)
    w_qa, w_kva, w_kpe, w_ga, w_z, w_xbc, w_dt = [w[:, offs[i]:offs[i + 1]] for i in range(len(MIX_SPLITS))]
    half = MLA_ROPE // 2
    kpe_pad = jnp.zeros((d, LANES), F32).at[:, MLA_NOPE:MLA_NOPE + MLA_ROPE].set(w_kpe)
    kpe_swap = jnp.concatenate([w_kpe[:, half:], w_kpe[:, :half]], axis=1)
    kps_pad = jnp.zeros((d, LANES), F32).at[:, MLA_NOPE:MLA_NOPE + MLA_ROPE].set(kpe_swap)
    dt_pad = jnp.pad(w_dt, ((0, 0), (0, LANES - w_dt.shape[1])))
    wn = jnp.concatenate([w_qa, w_kva, kpe_pad, kps_pad, w_z, w_xbc, dt_pad], axis=1).astype(BF16)
    wg_t = w_ga.T.astype(BF16)
    wuq = w_uq[0].reshape(MLA_Q_RANK, MLA_HEADS, MLA_NOPE + MLA_ROPE)
    wuq = jnp.pad(wuq, ((0, 0), (0, 0), (0, HEAD_PAD - MLA_NOPE - MLA_ROPE)))
    wuq_t = wuq.reshape(MLA_Q_RANK, MLA_HEADS * HEAD_PAD).T.astype(BF16)
    wukv = w_ukv[0].reshape(MLA_KV_RANK, MLA_HEADS, MLA_NOPE + MLA_V)
    wuk_p = jnp.pad(wukv[:, :, :MLA_NOPE], ((0, 0), (0, 0), (0, HEAD_PAD - MLA_NOPE)))
    wuk_p = wuk_p.reshape(MLA_KV_RANK, MLA_HEADS * HEAD_PAD).astype(BF16)
    wuv_t = wukv[:, :, MLA_NOPE:].reshape(MLA_KV_RANK, MLA_HEADS * MLA_V).T.astype(BF16)
    cos_q, sin_q, cos_k, sin_k = _rope_tables(n, nctx)

    q_t, kcat, v_t, g_t, sz, xbc, dt_raw = _mix_in(
        x, ctx, mod0, norm_w[0:1], wn, wg_t, q_norm[0:1], kv_norm[0:1], wuq_t, wuk_p, wuv_t,
        cos_q, sin_q, cos_k, sin_k, tm)

    a_t = _attention(q_t, kcat, v_t, g_t, tq=256)

    yf, yb = _ssd(xbc, dt_raw, conv_w[0], conv_b[0:1], _pad_row(a_log[0]), _pad_row(dt_bias[0]),
                  _pad_row(d_skip[0]), nctx)

    wo = w_out_mix[0].astype(BF16)
    x1, u, sg = _mix_out(a_t, yf, yb, sz, x, mod0, mod1, ssd_norm[0:1], wo[:MLA_WIDTH], wo[MLA_WIDTH:],
                         norm_w[1:2], w_in_pool[0].astype(BF16), tm)

    return _pool_out(u, sg, x1, mod1, pool_lin[0].astype(BF16), pool_scale[0:1], w_out_pool[0].astype(BF16),
                     final_norm.reshape(1, d), tm)
```

```python
import functools
import math

import jax
import jax.numpy as jnp
import numpy as np
from jax import lax
from jax.experimental import pallas as pl
from jax.experimental.pallas import tpu as pltpu

D_MODEL = 1024
GRID_W = 64

MLA_HEADS = 8
MLA_NOPE = 64
MLA_ROPE = 32
MLA_V = 64
MLA_Q_RANK = 384
MLA_KV_RANK = 256
MLA_WIDTH = MLA_HEADS * MLA_V
ROPE_BASE = 10000.0
HEAD_PAD = 128
V_ROWS = 80

SSD_HEADS = 8
SSD_HEAD_DIM = 64
SSD_WIDTH = SSD_HEADS * SSD_HEAD_DIM
SSD_GROUPS = 2
SSD_STATE = 64
SSD_CONV = 3
SSD_CHUNK = 128
SSD_CONV_DIM = SSD_WIDTH + 2 * SSD_GROUPS * SSD_STATE
MIX_SPLITS = (MLA_Q_RANK, MLA_KV_RANK, MLA_ROPE, MLA_WIDTH, SSD_WIDTH, SSD_CONV_DIM, 2 * SSD_HEADS)

POOL_WINDOWS = (2, 4, 8, 16)
POOL_GROUPS = len(POOL_WINDOWS)
POOL_WIDTH = D_MODEL
POOL_GROUP_DIM = POOL_WIDTH // POOL_GROUPS
POOL_HALO = 8

RMS_EPS = 1e-6
LANES = 128
SUBLANES = 8
VMEM_LIMIT = 56 * 1024 * 1024

C_QA = 0
C_KVA = C_QA + MLA_Q_RANK
C_KPE = C_KVA + MLA_KV_RANK
C_KPS = C_KPE + LANES
C_Z = C_KPS + LANES
C_XBC = C_Z + SSD_WIDTH
C_DT = C_XBC + SSD_CONV_DIM
C_END = C_DT + LANES

BF16 = jnp.bfloat16
F32 = jnp.float32
NEG_BIG = -1e30


def _rms(x, w):
    y = x * lax.rsqrt(jnp.mean(x * x, axis=-1, keepdims=True) + RMS_EPS)
    return y * w


def _silu(x):
    return x * (1.0 / (1.0 + jnp.exp(-x)))


def _softplus(x):
    return jnp.maximum(x, 0.0) + jnp.log(1.0 + jnp.exp(-jnp.abs(x)))


def _dot(a, b):
    return jnp.dot(a, b, preferred_element_type=F32)


def _dot_nt(a, b):
    return lax.dot_general(a, b, (((1,), (1,)), ((), ())), preferred_element_type=F32)


def _dot_tn(a, b):
    return lax.dot_general(a, b, (((0,), (0,)), ((), ())), preferred_element_type=F32)


def _params(sem):
    return pltpu.CompilerParams(dimension_semantics=sem, vmem_limit_bytes=VMEM_LIMIT)


def _mod_kernel(c_ref, w_ref, b_ref, o_ref):
    sc = _silu(c_ref[...]).astype(BF16)
    o_ref[0] = _dot(sc, w_ref[0].astype(BF16)) + b_ref[0]


def _modulation(cvec, mod_w, mod_b):
    depth, d, d3 = mod_w.shape
    rows = cvec.shape[0]
    nblk = d3 // d
    return pl.pallas_call(
        _mod_kernel,
        out_shape=jax.ShapeDtypeStruct((depth, rows, d3), F32),
        grid=(depth, nblk),
        in_specs=[
            pl.BlockSpec((rows, d), lambda i, j: (0, 0)),
            pl.BlockSpec((1, d, d), lambda i, j: (i, 0, j)),
            pl.BlockSpec((1, 1, d), lambda i, j: (i, 0, j)),
        ],
        out_specs=pl.BlockSpec((1, rows, d), lambda i, j: (i, 0, j)),
        compiler_params=_params(("arbitrary", "arbitrary")),
        name="modulation",
    )(cvec, mod_w, mod_b.reshape(depth, 1, d3))


def _mix_in_kernel(nct, nbatch, x_ref, ctx_ref, mod_ref, nw_ref, wn_ref, wg_ref, qn_ref, kvn_ref,
                   wuq_ref, wuk_ref, wuv_ref, cosq_ref, sinq_ref, cosk_ref, sink_ref,
                   q_ref, k_ref, v_ref, g_ref, sz_ref, xbc_ref, dt_ref):
    b = pl.program_id(0)
    t = pl.program_id(1)
    is_ctx = t < nct
    xin = jnp.where(is_ctx, ctx_ref[0], x_ref[0])
    row = jnp.where(is_ctx, nbatch, b)
    mod = mod_ref[pl.ds(row, 1), :]
    shift = mod[:, 0:D_MODEL]
    scale = mod[:, D_MODEL:2 * D_MODEL]
    h = _rms(xin, nw_ref[...]) * (1.0 + scale) + shift
    hb = h.astype(BF16)
    proj = _dot(hb, wn_ref[...])

    xbc_ref[0] = proj[:, C_XBC:C_DT]
    dt_ref[0] = proj[:, C_DT:C_END]

    ckv = _rms(proj[:, C_KVA:C_KPE], kvn_ref[...]).astype(BF16)
    kpad = _dot(ckv, wuk_ref[...])
    kpe = proj[:, C_KPE:C_KPS] * cosk_ref[...] + proj[:, C_KPS:C_Z] * sink_ref[...]
    vt = _dot_nt(wuv_ref[...], ckv)
    tm = vt.shape[1]
    ones_rows = (lax.broadcasted_iota(jnp.int32, (V_ROWS - MLA_V, tm), 0) == 0).astype(BF16)
    for hd in range(MLA_HEADS):
        k_ref[0, hd] = (kpad[:, hd * HEAD_PAD:(hd + 1) * HEAD_PAD] + kpe).astype(BF16)
        v_ref[0, hd, 0, 0:MLA_V, :] = vt[hd * MLA_V:(hd + 1) * MLA_V].astype(BF16)
        v_ref[0, hd, 0, MLA_V:V_ROWS, :] = ones_rows

    @pl.when(jnp.logical_not(is_ctx))
    def _():
        sz_ref[0] = _silu(proj[:, C_Z:C_XBC])
        g_ref[0] = _silu(_dot_nt(wg_ref[...], hb))
        cq = _rms(proj[:, C_QA:C_KVA], qn_ref[...]).astype(BF16)
        qt = _dot_nt(wuq_ref[...], cq)
        sm_scale = (MLA_NOPE + MLA_ROPE) ** -0.5 * math.log2(math.e)
        cos = cosq_ref[...]
        sin = sinq_ref[...]
        half = MLA_ROPE // 2
        for hd in range(MLA_HEADS):
            base = hd * HEAD_PAD
            t1 = qt[base + MLA_NOPE:base + MLA_NOPE + half]
            t2 = qt[base + MLA_NOPE + half:base + MLA_NOPE + MLA_ROPE]
            q_ref[0, hd, 0:MLA_NOPE, :] = (qt[base:base + MLA_NOPE] * sm_scale).astype(BF16)
            q_ref[0, hd, MLA_NOPE:MLA_NOPE + half, :] = ((t1 * cos - t2 * sin) * sm_scale).astype(BF16)
            q_ref[0, hd, MLA_NOPE + half:MLA_NOPE + MLA_ROPE, :] = ((t2 * cos + t1 * sin) * sm_scale).astype(BF16)
            q_ref[0, hd, MLA_NOPE + MLA_ROPE:HEAD_PAD, :] = jnp.zeros((HEAD_PAD - MLA_NOPE - MLA_ROPE, tm), BF16)


def _mix_in(x, ctx, mod0, norm_w, wn, wg_t, q_norm, kv_norm, wuq_t, wuk_p, wuv_t,
            cos_q, sin_q, cos_k, sin_k, tm):
    bsz, n, d = x.shape
    nctx = ctx.shape[1]
    nct = nctx // tm
    nlt = n // tm
    tt = nctx + n
    rows = mod0.shape[0]

    def lat(t):
        return jnp.maximum(t - nct, 0)

    const = lambda b, t: (0, 0)
    out_shape = (
        jax.ShapeDtypeStruct((bsz, MLA_HEADS, HEAD_PAD, n), BF16),
        jax.ShapeDtypeStruct((bsz, MLA_HEADS, tt, HEAD_PAD), BF16),
        jax.ShapeDtypeStruct((bsz, MLA_HEADS, tt // tm, V_ROWS, tm), BF16),
        jax.ShapeDtypeStruct((bsz, MLA_WIDTH, n), F32),
        jax.ShapeDtypeStruct((bsz, n, SSD_WIDTH), F32),
        jax.ShapeDtypeStruct((bsz, tt, SSD_CONV_DIM), F32),
        jax.ShapeDtypeStruct((bsz, tt, LANES), F32),
    )
    out_specs = (
        pl.BlockSpec((1, MLA_HEADS, HEAD_PAD, tm), lambda b, t: (b, 0, 0, lat(t))),
        pl.BlockSpec((1, MLA_HEADS, tm, HEAD_PAD), lambda b, t: (b, 0, t, 0)),
        pl.BlockSpec((1, MLA_HEADS, 1, V_ROWS, tm), lambda b, t: (b, 0, t, 0, 0)),
        pl.BlockSpec((1, MLA_WIDTH, tm), lambda b, t: (b, 0, lat(t))),
        pl.BlockSpec((1, tm, SSD_WIDTH), lambda b, t: (b, lat(t), 0)),
        pl.BlockSpec((1, tm, SSD_CONV_DIM), lambda b, t: (b, t, 0)),
        pl.BlockSpec((1, tm, LANES), lambda b, t: (b, t, 0)),
    )
    in_specs = [
        pl.BlockSpec((1, tm, d), lambda b, t: (b, lat(t), 0)),
        pl.BlockSpec((1, tm, d), lambda b, t: (b, jnp.minimum(t, nct - 1), 0)),
        pl.BlockSpec((rows, 3 * d), const),
        pl.BlockSpec((1, d), const),
        pl.BlockSpec(wn.shape, const),
        pl.BlockSpec(wg_t.shape, const),
        pl.BlockSpec((1, MLA_Q_RANK), const),
        pl.BlockSpec((1, MLA_KV_RANK), const),
        pl.BlockSpec(wuq_t.shape, const),
        pl.BlockSpec(wuk_p.shape, const),
        pl.BlockSpec(wuv_t.shape, const),
        pl.BlockSpec((MLA_ROPE // 2, tm), lambda b, t: (0, lat(t))),
        pl.BlockSpec((MLA_ROPE // 2, tm), lambda b, t: (0, lat(t))),
        pl.BlockSpec((tm, LANES), lambda b, t: (t, 0)),
        pl.BlockSpec((tm, LANES), lambda b, t: (t, 0)),
    ]
    return pl.pallas_call(
        functools.partial(_mix_in_kernel, nct, bsz),
        out_shape=out_shape,
        grid=(bsz, nct + nlt),
        in_specs=in_specs,
        out_specs=out_specs,
        compiler_params=_params(("arbitrary", "arbitrary")),
        name="mix_in",
    )(x, ctx, mod0, norm_w, wn, wg_t, q_norm, kv_norm, wuq_t, wuk_p, wuv_t, cos_q, sin_q, cos_k, sin_k)


def _attn_kernel(nkb, sub, kt, q_ref, k_ref, v_ref, g_ref, o_ref, sa_ref, sb_ref):
    tq = q_ref.shape[3]
    kb = sub * kt

    def scores(j, s_ref):
        ks = k_ref[0, 0, pl.ds(pl.multiple_of(j * kb, kb), kb), :]
        s = _dot(ks, q_ref[0, 0])
        s_ref[...] = s
        while s.shape[0] > SUBLANES and s.shape[0] % (2 * SUBLANES) == 0:
            half = s.shape[0] // 2
            s = jnp.maximum(s[:half], s[half:])
        return jnp.max(s, axis=0, keepdims=True)

    def softmax_pv(j, s_ref, mb, m, acc):
        mn = jnp.maximum(m, mb)
        alpha = jnp.exp2(m - mn)
        p = jnp.exp2(s_ref[...] - mn).astype(BF16)
        pv = _dot(v_ref[0, 0, j * sub], p[0:kt])
        for i in range(1, sub):
            pv = pv + _dot(v_ref[0, 0, j * sub + i], p[i * kt:(i + 1) * kt])
        return mn, acc * alpha + pv

    def body(jj, carry):
        mb, m, acc = carry
        j = 2 * jj
        mb1 = scores(j + 1, sb_ref)
        m, acc = softmax_pv(j, sa_ref, mb, m, acc)
        mb2 = scores(j + 2, sa_ref)
        m, acc = softmax_pv(j + 1, sb_ref, mb1, m, acc)
        return mb2, m, acc

    m0 = jnp.full((1, tq), NEG_BIG, F32)
    acc0 = jnp.zeros((V_ROWS, tq), F32)
    mb = scores(0, sa_ref)
    pairs = (nkb - 1) // 2
    mb, m, acc = lax.fori_loop(0, pairs, body, (mb, m0, acc0))
    j = 2 * pairs
    if j + 1 < nkb:
        mb1 = scores(j + 1, sb_ref)
        m, acc = softmax_pv(j, sa_ref, mb, m, acc)
        _, acc = softmax_pv(j + 1, sb_ref, mb1, m, acc)
    else:
        _, acc = softmax_pv(j, sa_ref, mb, m, acc)
    out = acc[0:MLA_V] / acc[MLA_V:MLA_V + 1]
    o_ref[0] = (out * g_ref[0]).astype(BF16)


def _attention(q_t, kcat, v_t, g_t, tq):
    bsz, nh, _, n = q_t.shape
    tt = kcat.shape[2]
    nvb, kt = v_t.shape[2], v_t.shape[4]
    sub = 3 if nvb % 3 == 0 else 1
    nkb = nvb // sub
    return pl.pallas_call(
        functools.partial(_attn_kernel, nkb, sub, kt),
        out_shape=jax.ShapeDtypeStruct((bsz, nh * MLA_V, n), BF16),
        grid=(bsz, nh, n // tq),
        in_specs=[
            pl.BlockSpec((1, 1, HEAD_PAD, tq), lambda b, h, i: (b, h, 0, i)),
            pl.BlockSpec((1, 1, tt, HEAD_PAD), lambda b, h, i: (b, h, 0, 0)),
            pl.BlockSpec((1, 1, nvb, V_ROWS, kt), lambda b, h, i: (b, h, 0, 0, 0)),
            pl.BlockSpec((1, MLA_V, tq), lambda b, h, i: (b, h, i)),
        ],
        out_specs=pl.BlockSpec((1, MLA_V, tq), lambda b, h, i: (b, h, i)),
        scratch_shapes=[pltpu.VMEM((sub * kt, tq), F32), pltpu.VMEM((sub * kt, tq), F32)],
        compiler_params=_params(("arbitrary", "arbitrary", "arbitrary")),
        name="attention",
    )(q_t, kcat, v_t, g_t)


def _ssd_direction(d, first_seg, last_seg, xc_ref, xp_ref, xn_ref, dt_ref, cw_ref, cb_ref, alog_ref,
                   dtb_ref, skip_ref, state_ref, y_ref, write_y):
    L = SSD_CHUNK
    xc = xc_ref[0]
    ridx = lax.broadcasted_iota(jnp.int32, (L, 1), 0)
    prev_row = jnp.where(first_seg, 0.0, xp_ref[0, SUBLANES - 1:SUBLANES, :])
    next_row = jnp.where(last_seg, 0.0, xn_ref[0, 0:1, :])
    x_m1 = jnp.where(ridx == 0, prev_row, pltpu.roll(xc, 1, axis=0))
    x_p1 = jnp.where(ridx == L - 1, next_row, pltpu.roll(xc, L - 1, axis=0))
    cw = cw_ref[...]
    conv = cb_ref[...] + x_m1 * cw[0:1] + xc * cw[1:2] + x_p1 * cw[2:3]
    u = _silu(conv)
    xs = u[:, 0:SSD_WIDTH]
    gn = SSD_GROUPS * SSD_STATE
    bm = u[:, SSD_WIDTH:SSD_WIDTH + gn].astype(BF16)
    cm = u[:, SSD_WIDTH + gn:SSD_WIDTH + 2 * gn].astype(BF16)

    dtm = _softplus(dt_ref[0] + dtb_ref[...])
    a = dtm * (-jnp.exp(alog_ref[...]))
    li = lax.broadcasted_iota(jnp.int32, (L, L), 0)
    si = lax.broadcasted_iota(jnp.int32, (L, L), 1)
    mask = (si <= li) if d == 0 else (si >= li)
    acs = jnp.dot(mask.astype(F32), a, preferred_element_type=F32, precision=lax.Precision.HIGHEST)
    acs_t = acs.T
    end = L - 1 if d == 0 else 0
    skip = skip_ref[...]

    heads_per_group = SSD_HEADS // SSD_GROUPS
    for g in range(SSD_GROUPS):
        bg = bm[:, g * SSD_STATE:(g + 1) * SSD_STATE]
        cg = cm[:, g * SSD_STATE:(g + 1) * SSD_STATE]
        cbm = _dot_nt(cg, bg)
        for r in range(heads_per_group):
            hd = g * heads_per_group + r
            col = d * SSD_HEADS + hd
            a_col = acs[:, col:col + 1]
            a_row = acs_t[col:col + 1, :]
            a_end = acs[end:end + 1, col:col + 1]
            xs_h = xs[:, hd * SSD_HEAD_DIM:(hd + 1) * SSD_HEAD_DIM]
            xd = xs_h * dtm[:, col:col + 1]
            h_in = state_ref[d, hd]
            if write_y is not None:
                lmat = jnp.exp(jnp.where(mask, a_col - a_row, -jnp.inf))
                y = _dot((cbm * lmat).astype(BF16), xd.astype(BF16))
                y = y + jnp.exp(a_col) * _dot_nt(cg, h_in.astype(BF16))
                y = y + skip[:, col:col + 1] * xs_h

                @pl.when(write_y)
                def _():
                    y_ref[0, :, hd * SSD_HEAD_DIM:(hd + 1) * SSD_HEAD_DIM] = y
            xdw = (xd * jnp.exp(a_end - a_col)).astype(BF16)
            state_ref[d, hd] = h_in * jnp.exp(a_end) + _dot_tn(xdw, bg)


def _ssd_kernel(ncc, nchunks, xf_ref, xfp_ref, xfn_ref, dtf_ref, xb_ref, xbp_ref, xbn_ref, dtb_raw_ref,
                cw_ref, cb_ref, alog_ref, dtbias_ref, skip_ref, yf_ref, yb_ref, state_ref):
    s = pl.program_id(1)

    @pl.when(s == 0)
    def _():
        state_ref[...] = jnp.zeros_like(state_ref)

    cf = s
    cbk = jnp.where(s < ncc, ncc - 1 - s, nchunks - 1 - (s - ncc))
    first = lambda c: jnp.logical_or(c == 0, c == ncc)
    last = lambda c: jnp.logical_or(c == ncc - 1, c == nchunks - 1)
    is_lat = s >= ncc
    _ssd_direction(0, first(cf), last(cf), xf_ref, xfp_ref, xfn_ref, dtf_ref, cw_ref, cb_ref, alog_ref,
                   dtbias_ref, skip_ref, state_ref, yf_ref, is_lat)
    _ssd_direction(1, first(cbk), last(cbk), xb_ref, xbp_ref, xbn_ref, dtb_raw_ref, cw_ref, cb_ref, alog_ref,
                   dtbias_ref, skip_ref, state_ref, yb_ref, is_lat)


def _ssd(xbc, dt_raw, conv_w, conv_b, alog_row, dtbias_row, skip_row, nctx):
    bsz, tt, cd = xbc.shape
    L = SSD_CHUNK
    nchunks = tt // L
    ncc = nctx // L
    n = tt - nctx
    r8 = L // SUBLANES
    nrow8 = tt // SUBLANES

    def cf(s):
        return s

    def cb(s):
        return jnp.where(s < ncc, ncc - 1 - s, nchunks - 1 - (s - ncc))

    def specs(cfn):
        return [
            pl.BlockSpec((1, L, cd), lambda b, s: (b, cfn(s), 0)),
            pl.BlockSpec((1, SUBLANES, cd), lambda b, s: (b, jnp.maximum(cfn(s) * r8 - 1, 0), 0)),
            pl.BlockSpec((1, SUBLANES, cd), lambda b, s: (b, jnp.minimum((cfn(s) + 1) * r8, nrow8 - 1), 0)),
            pl.BlockSpec((1, L, LANES), lambda b, s: (b, cfn(s), 0)),
        ]

    const = lambda b, s: (0, 0)
    in_specs = specs(cf) + specs(cb) + [
        pl.BlockSpec((SSD_CONV, cd), const),
        pl.BlockSpec((1, cd), const),
        pl.BlockSpec((1, LANES), const),
        pl.BlockSpec((1, LANES), const),
        pl.BlockSpec((1, LANES), const),
    ]
    out_specs = (
        pl.BlockSpec((1, L, SSD_WIDTH), lambda b, s: (b, jnp.maximum(cf(s) - ncc, 0), 0)),
        pl.BlockSpec((1, L, SSD_WIDTH), lambda b, s: (b, nchunks - 1 - ncc - jnp.maximum(s - ncc, 0), 0)),
    )
    y_shape = jax.ShapeDtypeStruct((bsz, n, SSD_WIDTH), F32)
    return pl.pallas_call(
        functools.partial(_ssd_kernel, ncc, nchunks),
        out_shape=(y_shape, y_shape),
        grid=(bsz, nchunks),
        in_specs=in_specs,
        out_specs=out_specs,
        scratch_shapes=[pltpu.VMEM((2, SSD_HEADS, SSD_HEAD_DIM, SSD_STATE), F32)],
        compiler_params=_params(("arbitrary", "arbitrary")),
        name="ssd",
    )(xbc, xbc, xbc, dt_raw, xbc, xbc, xbc, dt_raw, conv_w, conv_b, alog_row, dtbias_row, skip_row)


def _mix_out_kernel(nbatch, a_ref, yf_ref, yb_ref, sz_ref, x_ref, mod0_ref, mod1_ref, sn_ref, woa_ref, wos_ref,
                    nw1_ref, wp_ref, x1_ref, u_ref, sg_ref):
    b = pl.program_id(0)
    y = (yf_ref[0] + yb_ref[0]) * sz_ref[0]
    ssd = _rms(y, sn_ref[...]).astype(BF16)
    o = _dot_tn(a_ref[0], woa_ref[...]) + _dot(ssd, wos_ref[...])
    gate0 = mod0_ref[pl.ds(b, 1), 2 * D_MODEL:3 * D_MODEL]
    x1 = x_ref[0] + gate0 * o
    x1_ref[0] = x1
    mod1 = mod1_ref[pl.ds(b, 1), :]
    h1 = _rms(x1, nw1_ref[...]) * (1.0 + mod1[:, D_MODEL:2 * D_MODEL]) + mod1[:, 0:D_MODEL]
    ug = _dot(h1.astype(BF16), wp_ref[...])
    u_ref[0] = ug[:, 0:POOL_WIDTH]
    sg_ref[0] = _silu(ug[:, POOL_WIDTH:2 * POOL_WIDTH])


def _mix_out(a_t, yf, yb, sz, x, mod0, mod1, ssd_norm, wo_a, wo_s, norm_w1, w_pool, tm):
    bsz, n, d = x.shape
    rows = mod0.shape[0]
    const = lambda b, t: (0, 0)
    tile = lambda w: pl.BlockSpec((1, tm, w), lambda b, t: (b, t, 0))
    return pl.pallas_call(
        functools.partial(_mix_out_kernel, bsz),
        out_shape=(jax.ShapeDtypeStruct((bsz, n, d), F32),
                   jax.ShapeDtypeStruct((bsz, n, POOL_WIDTH), F32),
                   jax.ShapeDtypeStruct((bsz, n, POOL_WIDTH), F32)),
        grid=(bsz, n // tm),
        in_specs=[
            pl.BlockSpec((1, MLA_WIDTH, tm), lambda b, t: (b, 0, t)),
            tile(SSD_WIDTH), tile(SSD_WIDTH), tile(SSD_WIDTH), tile(d),
            pl.BlockSpec((rows, 3 * d), const),
            pl.BlockSpec((rows, 3 * d), const),
            pl.BlockSpec((1, SSD_WIDTH), const),
            pl.BlockSpec(wo_a.shape, const),
            pl.BlockSpec(wo_s.shape, const),
            pl.BlockSpec((1, d), const),
            pl.BlockSpec(w_pool.shape, const),
        ],
        out_specs=(tile(d), tile(POOL_WIDTH), tile(POOL_WIDTH)),
        compiler_params=_params(("arbitrary", "arbitrary")),
        name="mix_out",
    )(a_t, yf, yb, sz, x, mod0, mod1, ssd_norm, wo_a, wo_s, norm_w1, w_pool)


def _pool_kernel(n, u_ref, up_ref, un_ref, sg_ref, x1_ref, mod1_ref, lin_ref, ps_ref, wo_ref, fn_ref,
                 o_ref, ext_ref):
    b = pl.program_id(0)
    t = pl.program_id(1)
    nt = pl.num_programs(1)
    tm = u_ref.shape[1]
    H = POOL_HALO
    ext_ref[0:H, :] = jnp.where(t == 0, 0.0, up_ref[0])
    ext_ref[H:H + tm, :] = u_ref[0]
    ext_ref[H + tm:H + tm + H, :] = jnp.where(t == nt - 1, 0.0, un_ref[0])

    tok = t * tm + lax.broadcasted_iota(jnp.int32, (tm, 1), 0)
    ys = []
    for g, w in enumerate(POOL_WINDOWS):
        lanes = slice(g * POOL_GROUP_DIM, (g + 1) * POOL_GROUP_DIM)
        acc = ext_ref[H - w // 2:H - w // 2 + tm, lanes]
        for j in range(1, w):
            acc = acc + ext_ref[H - w // 2 + j:H - w // 2 + j + tm, lanes]
        lo = jnp.maximum(tok - w // 2, 0)
        hi = jnp.minimum(tok + (w - w // 2 - 1), n - 1)
        cnt = (hi - lo + 1).astype(F32)
        m = acc / cnt - u_ref[0, :, lanes]
        ys.append(_dot(m.astype(BF16), lin_ref[g]))
    y = jnp.concatenate(ys, axis=-1) * ps_ref[...] * sg_ref[0]
    o = _dot(y.astype(BF16), wo_ref[...])
    gate1 = mod1_ref[pl.ds(b, 1), 2 * D_MODEL:3 * D_MODEL]
    x2 = x1_ref[0] + gate1 * o
    o_ref[0] = _rms(x2, fn_ref[...])


def _pool_out(u, sg, x1, mod1, pool_lin, pool_scale, wo_pool, final_norm, tm):
    bsz, n, d = x1.shape
    rows = mod1.shape[0]
    r8 = tm // SUBLANES
    nrow8 = n // SUBLANES
    const = lambda b, t: (0, 0)
    tile = lambda w: pl.BlockSpec((1, tm, w), lambda b, t: (b, t, 0))
    return pl.pallas_call(
        functools.partial(_pool_kernel, n),
        out_shape=jax.ShapeDtypeStruct((bsz, n, d), F32),
        grid=(bsz, n // tm),
        in_specs=[
            tile(POOL_WIDTH),
            pl.BlockSpec((1, POOL_HALO, POOL_WIDTH), lambda b, t: (b, jnp.maximum(t * r8 - 1, 0), 0)),
            pl.BlockSpec((1, POOL_HALO, POOL_WIDTH), lambda b, t: (b, jnp.minimum((t + 1) * r8, nrow8 - 1), 0)),
            tile(POOL_WIDTH), tile(d),
            pl.BlockSpec((rows, 3 * d), const),
            pl.BlockSpec(pool_lin.shape, lambda b, t: (0, 0, 0)),
            pl.BlockSpec((1, POOL_WIDTH), const),
            pl.BlockSpec(wo_pool.shape, const),
            pl.BlockSpec((1, d), const),
        ],
        out_specs=tile(d),
        scratch_shapes=[pltpu.VMEM((tm + 2 * POOL_HALO, POOL_WIDTH), F32)],
        compiler_params=_params(("arbitrary", "arbitrary")),
        name="pool_out",
    )(u, u, u, sg, x1, mod1, pool_lin, pool_scale, wo_pool, final_norm)


def _rope_tables(n, nctx):
    rows = n // GRID_W
    row = jnp.repeat(jnp.arange(rows, dtype=F32), GRID_W)
    col = jnp.tile(jnp.arange(GRID_W, dtype=F32), rows)
    axis_dim = MLA_ROPE // 2
    inv_freq = 1.0 / (ROPE_BASE ** (jnp.arange(0, axis_dim, 2, dtype=F32) / axis_dim))
    ang = jnp.concatenate([row[:, None] * inv_freq, col[:, None] * inv_freq], axis=-1)
    cos, sin = jnp.cos(ang), jnp.sin(ang)
    half = MLA_ROPE // 2
    cos_k = jnp.zeros((nctx + n, LANES), F32)
    sin_k = jnp.zeros((nctx + n, LANES), F32)
    cos_all = jnp.concatenate([jnp.ones((nctx, half), F32), cos], axis=0)
    sin_all = jnp.concatenate([jnp.zeros((nctx, half), F32), sin], axis=0)
    cos_k = cos_k.at[:, MLA_NOPE:MLA_NOPE + half].set(cos_all).at[:, MLA_NOPE + half:MLA_NOPE + MLA_ROPE].set(cos_all)
    sin_k = sin_k.at[:, MLA_NOPE:MLA_NOPE + half].set(-sin_all).at[:, MLA_NOPE + half:MLA_NOPE + MLA_ROPE].set(sin_all)
    return cos.T, sin.T, cos_k, sin_k


def _pad_row(v):
    v = v.reshape(1, -1).astype(F32)
    return jnp.pad(v, ((0, 0), (0, LANES - v.shape[1])))


def kernel(x, c, ctx, c_ctx, mod_w, mod_b, norm_w, w_in_mix, q_norm, w_uq, kv_norm, w_ukv, conv_w, conv_b,
           a_log, dt_bias, d_skip, ssd_norm, w_out_mix, w_in_pool, pool_lin, pool_scale, w_out_pool, final_norm):
    bsz, n, d = x.shape
    nctx = ctx.shape[1]
    tm = 256
    assert d == D_MODEL and n % tm == 0 and nctx % tm == 0 and n % GRID_W == 0
    assert mod_w.shape[0] == 2, "two layers: one mixing layer followed by one pooling layer"

    rows = -(-(bsz + 1) // SUBLANES) * SUBLANES
    cvec = jnp.zeros((rows, d), F32).at[:bsz].set(c).at[bsz].set(c_ctx)
    mods = _modulation(cvec, mod_w, mod_b)
    mod0, mod1 = mods[0], mods[1]

    w = w_in_mix[0]
    offs = np.cumsum((0,) + MIX_SPLITS)
    w_qa, w_kva, w_kpe, w_ga, w_z, w_xbc, w_dt = [w[:, offs[i]:offs[i + 1]] for i in range(len(MIX_SPLITS))]
    half = MLA_ROPE // 2
    kpe_pad = jnp.zeros((d, LANES), F32).at[:, MLA_NOPE:MLA_NOPE + MLA_ROPE].set(w_kpe)
    kpe_swap = jnp.concatenate([w_kpe[:, half:], w_kpe[:, :half]], axis=1)
    kps_pad = jnp.zeros((d, LANES), F32).at[:, MLA_NOPE:MLA_NOPE + MLA_ROPE].set(kpe_swap)
    dt_pad = jnp.pad(w_dt, ((0, 0), (0, LANES - w_dt.shape[1])))
    wn = jnp.concatenate([w_qa, w_kva, kpe_pad, kps_pad, w_z, w_xbc, dt_pad], axis=1).astype(BF16)
    wg_t = w_ga.T.astype(BF16)
    wuq = w_uq[0].reshape(MLA_Q_RANK, MLA_HEADS, MLA_NOPE + MLA_ROPE)
    wuq = jnp.pad(wuq, ((0, 0), (0, 0), (0, HEAD_PAD - MLA_NOPE - MLA_ROPE)))
    wuq_t = wuq.reshape(MLA_Q_RANK, MLA_HEADS * HEAD_PAD).T.astype(BF16)
    wukv = w_ukv[0].reshape(MLA_KV_RANK, MLA_HEADS, MLA_NOPE + MLA_V)
    wuk_p = jnp.pad(wukv[:, :, :MLA_NOPE], ((0, 0), (0, 0), (0, HEAD_PAD - MLA_NOPE)))
    wuk_p = wuk_p.reshape(MLA_KV_RANK, MLA_HEADS * HEAD_PAD).astype(BF16)
    wuv_t = wukv[:, :, MLA_NOPE:].reshape(MLA_KV_RANK, MLA_HEADS * MLA_V).T.astype(BF16)
    cos_q, sin_q, cos_k, sin_k = _rope_tables(n, nctx)

    q_t, kcat, v_t, g_t, sz, xbc, dt_raw = _mix_in(
        x, ctx, mod0, norm_w[0:1], wn, wg_t, q_norm[0:1], kv_norm[0:1], wuq_t, wuk_p, wuv_t,
        cos_q, sin_q, cos_k, sin_k, tm)

    a_t = _attention(q_t, kcat, v_t, g_t, tq=512)

    yf, yb = _ssd(xbc, dt_raw, conv_w[0], conv_b[0:1], _pad_row(a_log[0]), _pad_row(dt_bias[0]),
                  _pad_row(d_skip[0]), nctx)

    wo = w_out_mix[0].astype(BF16)
    x1, u, sg = _mix_out(a_t, yf, yb, sz, x, mod0, mod1, ssd_norm[0:1], wo[:MLA_WIDTH], wo[MLA_WIDTH:],
                         norm_w[1:2], w_in_pool[0].astype(BF16), tm)

    return _pool_out(u, sg, x1, mod1, pool_lin[0].astype(BF16), pool_scale[0:1], w_out_pool[0].astype(BF16),
                     final_norm.reshape(1, d), tm)
```

```python
import functools
import math

import jax
import jax.numpy as jnp
import numpy as np
from jax import lax
from jax.experimental import pallas as pl
from jax.experimental.pallas import tpu as pltpu

D_MODEL = 1024
GRID_W = 64

MLA_HEADS = 8
MLA_NOPE = 64
MLA_ROPE = 32
MLA_V = 64
MLA_Q_RANK = 384
MLA_KV_RANK = 256
MLA_WIDTH = MLA_HEADS * MLA_V
ROPE_BASE = 10000.0
HEAD_PAD = 128
V_ROWS = 80

SSD_HEADS = 8
SSD_HEAD_DIM = 64
SSD_WIDTH = SSD_HEADS * SSD_HEAD_DIM
SSD_GROUPS = 2
SSD_STATE = 64
SSD_CONV = 3
SSD_CHUNK = 128
SSD_CONV_DIM = SSD_WIDTH + 2 * SSD_GROUPS * SSD_STATE
MIX_SPLITS = (MLA_Q_RANK, MLA_KV_RANK, MLA_ROPE, MLA_WIDTH, SSD_WIDTH, SSD_CONV_DIM, 2 * SSD_HEADS)
DT_COLS = 2 * SSD_HEADS
DT_REPL = 3

POOL_WINDOWS = (2, 4, 8, 16)
POOL_GROUPS = len(POOL_WINDOWS)
POOL_WIDTH = D_MODEL
POOL_GROUP_DIM = POOL_WIDTH // POOL_GROUPS
POOL_HALO = 8

RMS_EPS = 1e-6
LANES = 128
SUBLANES = 8
VMEM_LIMIT = 56 * 1024 * 1024

C_QA = 0
C_KVA = C_QA + MLA_Q_RANK
C_KPE = C_KVA + MLA_KV_RANK
C_KPS = C_KPE + LANES
C_Z = C_KPS + LANES
C_XBC = C_Z + SSD_WIDTH
C_DT = C_XBC + SSD_CONV_DIM
C_END = C_DT + LANES

BF16 = jnp.bfloat16
F32 = jnp.float32
NEG_BIG = -1e30


def _rms(x, w):
    y = x * lax.rsqrt(jnp.mean(x * x, axis=-1, keepdims=True) + RMS_EPS)
    return y * w


def _silu(x):
    return x * (1.0 / (1.0 + jnp.exp(-x)))


def _softplus(x):
    return jnp.maximum(x, 0.0) + jnp.log(1.0 + jnp.exp(-jnp.abs(x)))


def _dot(a, b):
    return jnp.dot(a, b, preferred_element_type=F32)


def _dot_nt(a, b):
    return lax.dot_general(a, b, (((1,), (1,)), ((), ())), preferred_element_type=F32)


def _dot_tn(a, b):
    return lax.dot_general(a, b, (((0,), (0,)), ((), ())), preferred_element_type=F32)


def _params(sem):
    return pltpu.CompilerParams(dimension_semantics=sem, vmem_limit_bytes=VMEM_LIMIT)


def _mod_kernel(c_ref, w_ref, b_ref, o_ref):
    sc = _silu(c_ref[...]).astype(BF16)
    o_ref[0] = _dot(sc, w_ref[0].astype(BF16)) + b_ref[0]


def _modulation(cvec, mod_w, mod_b):
    depth, d, d3 = mod_w.shape
    rows = cvec.shape[0]
    nblk = d3 // d
    return pl.pallas_call(
        _mod_kernel,
        out_shape=jax.ShapeDtypeStruct((depth, rows, d3), F32),
        grid=(depth, nblk),
        in_specs=[
            pl.BlockSpec((rows, d), lambda i, j: (0, 0)),
            pl.BlockSpec((1, d, d), lambda i, j: (i, 0, j)),
            pl.BlockSpec((1, 1, d), lambda i, j: (i, 0, j)),
        ],
        out_specs=pl.BlockSpec((1, rows, d), lambda i, j: (i, 0, j)),
        compiler_params=_params(("arbitrary", "arbitrary")),
        name="modulation",
    )(cvec, mod_w, mod_b.reshape(depth, 1, d3))


def _mix_in_kernel(nct, nbatch, x_ref, ctx_ref, mod_ref, nw_ref, wn_ref, wg_ref, qn_ref, kvn_ref,
                   wuq_ref, wuk_ref, wuv_ref, cosq_ref, sinq_ref, cosk_ref, sink_ref,
                   q_ref, k_ref, v_ref, g_ref, sz_ref, xbc_ref, dt_ref):
    b = pl.program_id(0)
    t = pl.program_id(1)
    is_ctx = t < nct
    xin = jnp.where(is_ctx, ctx_ref[0], x_ref[0])
    row = jnp.where(is_ctx, nbatch, b)
    mod = mod_ref[pl.ds(row, 1), :]
    shift = mod[:, 0:D_MODEL]
    scale = mod[:, D_MODEL:2 * D_MODEL]
    h = _rms(xin, nw_ref[...]) * (1.0 + scale) + shift
    hb = h.astype(BF16)
    proj = _dot(hb, wn_ref[...])

    xbc_ref[0] = proj[:, C_XBC:C_DT]
    dt_ref[0] = proj[:, C_DT:C_END]

    ckv = _rms(proj[:, C_KVA:C_KPE], kvn_ref[...]).astype(BF16)
    kpad = _dot(ckv, wuk_ref[...])
    kpe = proj[:, C_KPE:C_KPS] * cosk_ref[...] + proj[:, C_KPS:C_Z] * sink_ref[...]
    vt = _dot_nt(wuv_ref[...], ckv)
    tm = vt.shape[1]
    ones_rows = (lax.broadcasted_iota(jnp.int32, (V_ROWS - MLA_V, tm), 0) == 0).astype(BF16)
    for hd in range(MLA_HEADS):
        k_ref[0, hd] = (kpad[:, hd * HEAD_PAD:(hd + 1) * HEAD_PAD] + kpe).astype(BF16)
        v_ref[0, hd, 0, 0:MLA_V, :] = vt[hd * MLA_V:(hd + 1) * MLA_V].astype(BF16)
        v_ref[0, hd, 0, MLA_V:V_ROWS, :] = ones_rows

    @pl.when(jnp.logical_not(is_ctx))
    def _():
        sz_ref[0] = _silu(proj[:, C_Z:C_XBC])
        g_ref[0] = _silu(_dot_nt(wg_ref[...], hb))
        cq = _rms(proj[:, C_QA:C_KVA], qn_ref[...]).astype(BF16)
        qt = _dot_nt(wuq_ref[...], cq)
        sm_scale = (MLA_NOPE + MLA_ROPE) ** -0.5 * math.log2(math.e)
        cos = cosq_ref[...]
        sin = sinq_ref[...]
        half = MLA_ROPE // 2
        for hd in range(MLA_HEADS):
            base = hd * HEAD_PAD
            t1 = qt[base + MLA_NOPE:base + MLA_NOPE + half]
            t2 = qt[base + MLA_NOPE + half:base + MLA_NOPE + MLA_ROPE]
            q_ref[0, hd, 0:MLA_NOPE, :] = (qt[base:base + MLA_NOPE] * sm_scale).astype(BF16)
            q_ref[0, hd, MLA_NOPE:MLA_NOPE + half, :] = ((t1 * cos - t2 * sin) * sm_scale).astype(BF16)
            q_ref[0, hd, MLA_NOPE + half:MLA_NOPE + MLA_ROPE, :] = ((t2 * cos + t1 * sin) * sm_scale).astype(BF16)
            q_ref[0, hd, MLA_NOPE + MLA_ROPE:HEAD_PAD, :] = jnp.zeros((HEAD_PAD - MLA_NOPE - MLA_ROPE, tm), BF16)


def _mix_in(x, ctx, mod0, norm_w, wn, wg_t, q_norm, kv_norm, wuq_t, wuk_p, wuv_t,
            cos_q, sin_q, cos_k, sin_k, tm):
    bsz, n, d = x.shape
    nctx = ctx.shape[1]
    nct = nctx // tm
    nlt = n // tm
    tt = nctx + n
    rows = mod0.shape[0]

    def lat(t):
        return jnp.maximum(t - nct, 0)

    const = lambda b, t: (0, 0)
    out_shape = (
        jax.ShapeDtypeStruct((bsz, MLA_HEADS, HEAD_PAD, n), BF16),
        jax.ShapeDtypeStruct((bsz, MLA_HEADS, tt, HEAD_PAD), BF16),
        jax.ShapeDtypeStruct((bsz, MLA_HEADS, tt // tm, V_ROWS, tm), BF16),
        jax.ShapeDtypeStruct((bsz, MLA_WIDTH, n), F32),
        jax.ShapeDtypeStruct((bsz, n, SSD_WIDTH), F32),
        jax.ShapeDtypeStruct((bsz, tt, SSD_CONV_DIM), F32),
        jax.ShapeDtypeStruct((bsz, tt, LANES), F32),
    )
    out_specs = (
        pl.BlockSpec((1, MLA_HEADS, HEAD_PAD, tm), lambda b, t: (b, 0, 0, lat(t))),
        pl.BlockSpec((1, MLA_HEADS, tm, HEAD_PAD), lambda b, t: (b, 0, t, 0)),
        pl.BlockSpec((1, MLA_HEADS, 1, V_ROWS, tm), lambda b, t: (b, 0, t, 0, 0)),
        pl.BlockSpec((1, MLA_WIDTH, tm), lambda b, t: (b, 0, lat(t))),
        pl.BlockSpec((1, tm, SSD_WIDTH), lambda b, t: (b, lat(t), 0)),
        pl.BlockSpec((1, tm, SSD_CONV_DIM), lambda b, t: (b, t, 0)),
        pl.BlockSpec((1, tm, LANES), lambda b, t: (b, t, 0)),
    )
    in_specs = [
        pl.BlockSpec((1, tm, d), lambda b, t: (b, lat(t), 0)),
        pl.BlockSpec((1, tm, d), lambda b, t: (b, jnp.minimum(t, nct - 1), 0)),
        pl.BlockSpec((rows, 3 * d), const),
        pl.BlockSpec((1, d), const),
        pl.BlockSpec(wn.shape, const),
        pl.BlockSpec(wg_t.shape, const),
        pl.BlockSpec((1, MLA_Q_RANK), const),
        pl.BlockSpec((1, MLA_KV_RANK), const),
        pl.BlockSpec(wuq_t.shape, const),
        pl.BlockSpec(wuk_p.shape, const),
        pl.BlockSpec(wuv_t.shape, const),
        pl.BlockSpec((MLA_ROPE // 2, tm), lambda b, t: (0, lat(t))),
        pl.BlockSpec((MLA_ROPE // 2, tm), lambda b, t: (0, lat(t))),
        pl.BlockSpec((tm, LANES), lambda b, t: (t, 0)),
        pl.BlockSpec((tm, LANES), lambda b, t: (t, 0)),
    ]
    return pl.pallas_call(
        functools.partial(_mix_in_kernel, nct, bsz),
        out_shape=out_shape,
        grid=(bsz, nct + nlt),
        in_specs=in_specs,
        out_specs=out_specs,
        compiler_params=_params(("arbitrary", "arbitrary")),
        name="mix_in",
    )(x, ctx, mod0, norm_w, wn, wg_t, q_norm, kv_norm, wuq_t, wuk_p, wuv_t, cos_q, sin_q, cos_k, sin_k)


def _attn_kernel(nkb, sub, kt, q_ref, k_ref, v_ref, g_ref, o_ref, sa_ref, sb_ref):
    tq = q_ref.shape[3]
    kb = sub * kt

    def scores(j, s_ref):
        ks = k_ref[0, 0, pl.ds(pl.multiple_of(j * kb, kb), kb), :]
        s = _dot(ks, q_ref[0, 0])
        s_ref[...] = s
        while s.shape[0] > SUBLANES and s.shape[0] % (2 * SUBLANES) == 0:
            half = s.shape[0] // 2
            s = jnp.maximum(s[:half], s[half:])
        return jnp.max(s, axis=0, keepdims=True)

    def softmax_pv(j, s_ref, mb, m, acc):
        mn = jnp.maximum(m, mb)
        alpha = jnp.exp2(m - mn)
        p = jnp.exp2(s_ref[...] - mn).astype(BF16)
        pv = _dot(v_ref[0, 0, j * sub], p[0:kt])
        for i in range(1, sub):
            pv = pv + _dot(v_ref[0, 0, j * sub + i], p[i * kt:(i + 1) * kt])
        return mn, acc * alpha + pv

    def body(jj, carry):
        mb, m, acc = carry
        j = 2 * jj
        mb1 = scores(j + 1, sb_ref)
        m, acc = softmax_pv(j, sa_ref, mb, m, acc)
        mb2 = scores(j + 2, sa_ref)
        m, acc = softmax_pv(j + 1, sb_ref, mb1, m, acc)
        return mb2, m, acc

    m0 = jnp.full((1, tq), NEG_BIG, F32)
    acc0 = jnp.zeros((V_ROWS, tq), F32)
    mb = scores(0, sa_ref)
    pairs = (nkb - 1) // 2
    mb, m, acc = lax.fori_loop(0, pairs, body, (mb, m0, acc0))
    j = 2 * pairs
    if j + 1 < nkb:
        mb1 = scores(j + 1, sb_ref)
        m, acc = softmax_pv(j, sa_ref, mb, m, acc)
        _, acc = softmax_pv(j + 1, sb_ref, mb1, m, acc)
    else:
        _, acc = softmax_pv(j, sa_ref, mb, m, acc)
    out = acc[0:MLA_V] / acc[MLA_V:MLA_V + 1]
    o_ref[0] = (out * g_ref[0]).astype(BF16)


def _attention(q_t, kcat, v_t, g_t, tq):
    bsz, nh, _, n = q_t.shape
    tt = kcat.shape[2]
    nvb, kt = v_t.shape[2], v_t.shape[4]
    sub = 3 if nvb % 3 == 0 else 1
    nkb = nvb // sub
    return pl.pallas_call(
        functools.partial(_attn_kernel, nkb, sub, kt),
        out_shape=jax.ShapeDtypeStruct((bsz, nh * MLA_V, n), BF16),
        grid=(bsz, nh, n // tq),
        in_specs=[
            pl.BlockSpec((1, 1, HEAD_PAD, tq), lambda b, h, i: (b, h, 0, i)),
            pl.BlockSpec((1, 1, tt, HEAD_PAD), lambda b, h, i: (b, h, 0, 0)),
            pl.BlockSpec((1, 1, nvb, V_ROWS, kt), lambda b, h, i: (b, h, 0, 0, 0)),
            pl.BlockSpec((1, MLA_V, tq), lambda b, h, i: (b, h, i)),
        ],
        out_specs=pl.BlockSpec((1, MLA_V, tq), lambda b, h, i: (b, h, i)),
        scratch_shapes=[pltpu.VMEM((sub * kt, tq), F32), pltpu.VMEM((sub * kt, tq), F32)],
        compiler_params=_params(("arbitrary", "arbitrary", "arbitrary")),
        name="attention",
    )(q_t, kcat, v_t, g_t)


def _ssd_direction(d, first_seg, last_seg, xc_ref, xp_ref, xn_ref, dtm, acs, cw_ref, cb_ref,
                   skip_ref, eh_ref, ew_ref, state_ref, y_ref):
    L = SSD_CHUNK
    xc = xc_ref[0]
    ridx = lax.broadcasted_iota(jnp.int32, (L, 1), 0)
    prev_row = jnp.where(first_seg, 0.0, xp_ref[0, SUBLANES - 1:SUBLANES, :])
    next_row = jnp.where(last_seg, 0.0, xn_ref[0, 0:1, :])
    x_m1 = jnp.where(ridx == 0, prev_row, pltpu.roll(xc, 1, axis=0))
    x_p1 = jnp.where(ridx == L - 1, next_row, pltpu.roll(xc, L - 1, axis=0))
    cw = cw_ref[...]
    conv = cb_ref[...] + x_m1 * cw[0:1] + xc * cw[1:2] + x_p1 * cw[2:3]
    u = _silu(conv)
    xs = u[:, 0:SSD_WIDTH]
    gn = SSD_GROUPS * SSD_STATE
    bm = u[:, SSD_WIDTH:SSD_WIDTH + gn].astype(BF16)
    cm = u[:, SSD_WIDTH + gn:SSD_WIDTH + 2 * gn].astype(BF16)

    e_head = eh_ref[d]
    e_wide = ew_ref[d]
    dt_x = _dot(_split3(dtm), e_head)
    acs3 = _split3(acs)
    acs_x = _dot(acs3, e_head)
    acs_w = _dot(acs3, e_wide)
    acs_t = acs.T
    end = L - 1 if d == 0 else 0
    a_end = acs_x[end:end + 1, :]

    li = lax.broadcasted_iota(jnp.int32, (L, L), 0)
    si = lax.broadcasted_iota(jnp.int32, (L, L), 1)
    mask = (si <= li) if d == 0 else (si >= li)
    lane = lax.broadcasted_iota(jnp.int32, (L, LANES), 1)
    low = lane < SSD_STATE
    zero_b = jnp.zeros((L, LANES), BF16)

    xd = xs * dt_x
    xdb = xd.astype(BF16)
    heads_per_group = SSD_HEADS // SSD_GROUPS
    ys = []
    for g in range(SSD_GROUPS):
        cg = jnp.where(low, cm, zero_b) if g == 0 else jnp.where(low, zero_b, cm)
        cbm = _dot_nt(cg, bm)
        for pr in range(heads_per_group // 2):
            ms = []
            for r in range(2):
                hd = g * heads_per_group + 2 * pr + r
                col = d * SSD_HEADS + hd
                diff = acs_w[:, hd * L:(hd + 1) * L] - acs_t[col:col + 1, :]
                lmat = jnp.exp(jnp.where(mask, diff, -jnp.inf))
                ms.append((cbm * lmat).astype(BF16))
            pair = (g * heads_per_group) // 2 + pr
            xp = xdb[:, pair * LANES:(pair + 1) * LANES]
            rhs = jnp.concatenate([jnp.where(low, xp, zero_b), jnp.where(low, zero_b, xp)], axis=0)
            ys.append(_dot(jnp.concatenate(ms, axis=1), rhs))
    h_in = state_ref[d]
    y = jnp.concatenate(ys, axis=1) + jnp.exp(acs_x) * _dot(cm, h_in.astype(BF16)) + skip_ref[d:d + 1, :] * xs
    y_ref[0] = y

    xdw = (xd * jnp.exp(a_end - acs_x)).astype(BF16)
    st = _dot_tn(bm, xdw)
    gi = lax.broadcasted_iota(jnp.int32, st.shape, 0) // SSD_STATE
    hi = lax.broadcasted_iota(jnp.int32, st.shape, 1) // (heads_per_group * SSD_HEAD_DIM)
    state_ref[d] = h_in * jnp.exp(a_end) + jnp.where(gi == hi, st, 0.0)


def _split3(x):
    lane = lax.broadcasted_iota(jnp.int32, x.shape, 1)
    r1 = x - x.astype(BF16).astype(F32)
    r2 = r1 - r1.astype(BF16).astype(F32)
    return jnp.where(lane < DT_COLS, x, jnp.where(lane < 2 * DT_COLS, r1, r2)).astype(BF16)


def _ssd_kernel(ncc, nchunks, xf_ref, xfp_ref, xfn_ref, dtf_ref, xb_ref, xbp_ref, xbn_ref, dtb_raw_ref,
                cw_ref, cb_ref, alog_ref, dtbias_ref, skip_ref, tri_ref, eh_ref, ew_ref,
                yf_ref, yb_ref, state_ref):
    s = pl.program_id(1)
    L = SSD_CHUNK

    @pl.when(s == 0)
    def _():
        state_ref[...] = jnp.zeros_like(state_ref)

    cf = s
    cbk = jnp.where(s < ncc, ncc - 1 - s, nchunks - 1 - (s - ncc))
    first = lambda c: jnp.logical_or(c == 0, c == ncc)
    last = lambda c: jnp.logical_or(c == ncc - 1, c == nchunks - 1)

    neg_a = -jnp.exp(alog_ref[...])
    dtm_f = _softplus(dtf_ref[0] + dtbias_ref[...])
    dtm_b = _softplus(dtb_raw_ref[0] + dtbias_ref[...])
    a_f = dtm_f * neg_a
    a_b = dtm_b * neg_a
    cs = jnp.dot(tri_ref[...], jnp.concatenate([a_f, a_b], axis=1), preferred_element_type=F32,
                 precision=lax.Precision.HIGHEST)
    acs_f = cs[:, 0:LANES]
    cs_b = cs[:, LANES:2 * LANES]
    acs_b = cs_b[L - 1:L, :] - cs_b + a_b

    _ssd_direction(0, first(cf), last(cf), xf_ref, xfp_ref, xfn_ref, dtm_f, acs_f, cw_ref, cb_ref,
                   skip_ref, eh_ref, ew_ref, state_ref, yf_ref)
    _ssd_direction(1, first(cbk), last(cbk), xb_ref, xbp_ref, xbn_ref, dtm_b, acs_b, cw_ref, cb_ref,
                   skip_ref, eh_ref, ew_ref, state_ref, yb_ref)


def _ssd_constants():
    L = SSD_CHUNK
    tri = np.tril(np.ones((L, L), np.float32))
    e_head = np.zeros((2, LANES, SSD_WIDTH), np.float32)
    e_wide = np.zeros((2, LANES, SSD_HEADS * L), np.float32)
    for d in range(2):
        for rep in range(DT_REPL):
            for h in range(SSD_HEADS):
                r = rep * DT_COLS + d * SSD_HEADS + h
                e_head[d, r, h * SSD_HEAD_DIM:(h + 1) * SSD_HEAD_DIM] = 1.0
                e_wide[d, r, h * L:(h + 1) * L] = 1.0
    return jnp.asarray(tri), jnp.asarray(e_head, BF16), jnp.asarray(e_wide, BF16)


def _ssd(xbc, dt_raw, conv_w, conv_b, alog_row, dtbias_row, skip_x, nctx):
    bsz, tt, cd = xbc.shape
    L = SSD_CHUNK
    tri, e_head, e_wide = _ssd_constants()
    nchunks = tt // L
    ncc = nctx // L
    n = tt - nctx
    r8 = L // SUBLANES
    nrow8 = tt // SUBLANES

    def cf(s):
        return s

    def cb(s):
        return jnp.where(s < ncc, ncc - 1 - s, nchunks - 1 - (s - ncc))

    def specs(cfn):
        return [
            pl.BlockSpec((1, L, cd), lambda b, s: (b, cfn(s), 0)),
            pl.BlockSpec((1, SUBLANES, cd), lambda b, s: (b, jnp.maximum(cfn(s) * r8 - 1, 0), 0)),
            pl.BlockSpec((1, SUBLANES, cd), lambda b, s: (b, jnp.minimum((cfn(s) + 1) * r8, nrow8 - 1), 0)),
            pl.BlockSpec((1, L, LANES), lambda b, s: (b, cfn(s), 0)),
        ]

    const = lambda b, s: (0, 0)
    in_specs = specs(cf) + specs(cb) + [
        pl.BlockSpec((SSD_CONV, cd), const),
        pl.BlockSpec((1, cd), const),
        pl.BlockSpec((1, LANES), const),
        pl.BlockSpec((1, LANES), const),
        pl.BlockSpec(skip_x.shape, const),
        pl.BlockSpec(tri.shape, const),
        pl.BlockSpec(e_head.shape, lambda b, s: (0, 0, 0)),
        pl.BlockSpec(e_wide.shape, lambda b, s: (0, 0, 0)),
    ]
    out_specs = (
        pl.BlockSpec((1, L, SSD_WIDTH), lambda b, s: (b, jnp.maximum(cf(s) - ncc, 0), 0)),
        pl.BlockSpec((1, L, SSD_WIDTH), lambda b, s: (b, nchunks - 1 - ncc - jnp.maximum(s - ncc, 0), 0)),
    )
    y_shape = jax.ShapeDtypeStruct((bsz, n, SSD_WIDTH), F32)
    return pl.pallas_call(
        functools.partial(_ssd_kernel, ncc, nchunks),
        out_shape=(y_shape, y_shape),
        grid=(bsz, nchunks),
        in_specs=in_specs,
        out_specs=out_specs,
        scratch_shapes=[pltpu.VMEM((2, SSD_GROUPS * SSD_STATE, SSD_WIDTH), F32)],
        compiler_params=_params(("arbitrary", "arbitrary")),
        name="ssd",
    )(xbc, xbc, xbc, dt_raw, xbc, xbc, xbc, dt_raw, conv_w, conv_b, alog_row, dtbias_row, skip_x,
      tri, e_head, e_wide)


def _mix_out_kernel(nbatch, a_ref, yf_ref, yb_ref, sz_ref, x_ref, mod0_ref, mod1_ref, sn_ref, woa_ref, wos_ref,
                    nw1_ref, wp_ref, x1_ref, u_ref, sg_ref):
    b = pl.program_id(0)
    y = (yf_ref[0] + yb_ref[0]) * sz_ref[0]
    ssd = _rms(y, sn_ref[...]).astype(BF16)
    o = _dot_tn(a_ref[0], woa_ref[...]) + _dot(ssd, wos_ref[...])
    gate0 = mod0_ref[pl.ds(b, 1), 2 * D_MODEL:3 * D_MODEL]
    x1 = x_ref[0] + gate0 * o
    x1_ref[0] = x1
    mod1 = mod1_ref[pl.ds(b, 1), :]
    h1 = _rms(x1, nw1_ref[...]) * (1.0 + mod1[:, D_MODEL:2 * D_MODEL]) + mod1[:, 0:D_MODEL]
    ug = _dot(h1.astype(BF16), wp_ref[...])
    u_ref[0] = ug[:, 0:POOL_WIDTH]
    sg_ref[0] = _silu(ug[:, POOL_WIDTH:2 * POOL_WIDTH])


def _mix_out(a_t, yf, yb, sz, x, mod0, mod1, ssd_norm, wo_a, wo_s, norm_w1, w_pool, tm):
    bsz, n, d = x.shape
    rows = mod0.shape[0]
    const = lambda b, t: (0, 0)
    tile = lambda w: pl.BlockSpec((1, tm, w), lambda b, t: (b, t, 0))
    return pl.pallas_call(
        functools.partial(_mix_out_kernel, bsz),
        out_shape=(jax.ShapeDtypeStruct((bsz, n, d), F32),
                   jax.ShapeDtypeStruct((bsz, n, POOL_WIDTH), F32),
                   jax.ShapeDtypeStruct((bsz, n, POOL_WIDTH), F32)),
        grid=(bsz, n // tm),
        in_specs=[
            pl.BlockSpec((1, MLA_WIDTH, tm), lambda b, t: (b, 0, t)),
            tile(SSD_WIDTH), tile(SSD_WIDTH), tile(SSD_WIDTH), tile(d),
            pl.BlockSpec((rows, 3 * d), const),
            pl.BlockSpec((rows, 3 * d), const),
            pl.BlockSpec((1, SSD_WIDTH), const),
            pl.BlockSpec(wo_a.shape, const),
            pl.BlockSpec(wo_s.shape, const),
            pl.BlockSpec((1, d), const),
            pl.BlockSpec(w_pool.shape, const),
        ],
        out_specs=(tile(d), tile(POOL_WIDTH), tile(POOL_WIDTH)),
        compiler_params=_params(("arbitrary", "arbitrary")),
        name="mix_out",
    )(a_t, yf, yb, sz, x, mod0, mod1, ssd_norm, wo_a, wo_s, norm_w1, w_pool)


def _pool_kernel(n, u_ref, up_ref, un_ref, sg_ref, x1_ref, mod1_ref, lin_ref, ps_ref, wo_ref, fn_ref,
                 o_ref, ext_ref):
    b = pl.program_id(0)
    t = pl.program_id(1)
    nt = pl.num_programs(1)
    tm = u_ref.shape[1]
    H = POOL_HALO
    ext_ref[0:H, :] = jnp.where(t == 0, 0.0, up_ref[0])
    ext_ref[H:H + tm, :] = u_ref[0]
    ext_ref[H + tm:H + tm + H, :] = jnp.where(t == nt - 1, 0.0, un_ref[0])

    tok = t * tm + lax.broadcasted_iota(jnp.int32, (tm, 1), 0)
    ys = []
    for g, w in enumerate(POOL_WINDOWS):
        lanes = slice(g * POOL_GROUP_DIM, (g + 1) * POOL_GROUP_DIM)
        acc = ext_ref[H - w // 2:H - w // 2 + tm, lanes]
        for j in range(1, w):
            acc = acc + ext_ref[H - w // 2 + j:H - w // 2 + j + tm, lanes]
        lo = jnp.maximum(tok - w // 2, 0)
        hi = jnp.minimum(tok + (w - w // 2 - 1), n - 1)
        cnt = (hi - lo + 1).astype(F32)
        m = acc / cnt - u_ref[0, :, lanes]
        ys.append(_dot(m.astype(BF16), lin_ref[g]))
    y = jnp.concatenate(ys, axis=-1) * ps_ref[...] * sg_ref[0]
    o = _dot(y.astype(BF16), wo_ref[...])
    gate1 = mod1_ref[pl.ds(b, 1), 2 * D_MODEL:3 * D_MODEL]
    x2 = x1_ref[0] + gate1 * o
    o_ref[0] = _rms(x2, fn_ref[...])


def _pool_out(u, sg, x1, mod1, pool_lin, pool_scale, wo_pool, final_norm, tm):
    bsz, n, d = x1.shape
    rows = mod1.shape[0]
    r8 = tm // SUBLANES
    nrow8 = n // SUBLANES
    const = lambda b, t: (0, 0)
    tile = lambda w: pl.BlockSpec((1, tm, w), lambda b, t: (b, t, 0))
    return pl.pallas_call(
        functools.partial(_pool_kernel, n),
        out_shape=jax.ShapeDtypeStruct((bsz, n, d), F32),
        grid=(bsz, n // tm),
        in_specs=[
            tile(POOL_WIDTH),
            pl.BlockSpec((1, POOL_HALO, POOL_WIDTH), lambda b, t: (b, jnp.maximum(t * r8 - 1, 0), 0)),
            pl.BlockSpec((1, POOL_HALO, POOL_WIDTH), lambda b, t: (b, jnp.minimum((t + 1) * r8, nrow8 - 1), 0)),
            tile(POOL_WIDTH), tile(d),
            pl.BlockSpec((rows, 3 * d), const),
            pl.BlockSpec(pool_lin.shape, lambda b, t: (0, 0, 0)),
            pl.BlockSpec((1, POOL_WIDTH), const),
            pl.BlockSpec(wo_pool.shape, const),
            pl.BlockSpec((1, d), const),
        ],
        out_specs=tile(d),
        scratch_shapes=[pltpu.VMEM((tm + 2 * POOL_HALO, POOL_WIDTH), F32)],
        compiler_params=_params(("arbitrary", "arbitrary")),
        name="pool_out",
    )(u, u, u, sg, x1, mod1, pool_lin, pool_scale, wo_pool, final_norm)


def _rope_tables(n, nctx):
    rows = n // GRID_W
    row = jnp.repeat(jnp.arange(rows, dtype=F32), GRID_W)
    col = jnp.tile(jnp.arange(GRID_W, dtype=F32), rows)
    axis_dim = MLA_ROPE // 2
    inv_freq = 1.0 / (ROPE_BASE ** (jnp.arange(0, axis_dim, 2, dtype=F32) / axis_dim))
    ang = jnp.concatenate([row[:, None] * inv_freq, col[:, None] * inv_freq], axis=-1)
    cos, sin = jnp.cos(ang), jnp.sin(ang)
    half = MLA_ROPE // 2
    cos_all = jnp.concatenate([jnp.ones((nctx, half), F32), cos], axis=0)
    sin_all = jnp.concatenate([jnp.zeros((nctx, half), F32), sin], axis=0)
    left = jnp.zeros((nctx + n, MLA_NOPE), F32)
    right = jnp.zeros((nctx + n, LANES - MLA_NOPE - MLA_ROPE), F32)
    cos_k = jnp.concatenate([left, cos_all, cos_all, right], axis=1)
    sin_k = jnp.concatenate([left, -sin_all, sin_all, right], axis=1)
    return cos.T, sin.T, cos_k, sin_k


def _pad_row(v):
    v = v.reshape(1, -1).astype(F32)
    return jnp.pad(v, ((0, 0), (0, LANES - v.shape[1])))


def kernel(x, c, ctx, c_ctx, mod_w, mod_b, norm_w, w_in_mix, q_norm, w_uq, kv_norm, w_ukv, conv_w, conv_b,
           a_log, dt_bias, d_skip, ssd_norm, w_out_mix, w_in_pool, pool_lin, pool_scale, w_out_pool, final_norm):
    bsz, n, d = x.shape
    nctx = ctx.shape[1]
    tm = 256
    assert d == D_MODEL and n % tm == 0 and nctx % tm == 0 and n % GRID_W == 0
    assert mod_w.shape[0] == 2, "two layers: one mixing layer followed by one pooling layer"

    rows = -(-(bsz + 1) // SUBLANES) * SUBLANES
    cvec = jnp.zeros((rows, d), F32).at[:bsz].set(c).at[bsz].set(c_ctx)
    mods = _modulation(cvec, mod_w, mod_b)
    mod0, mod1 = mods[0], mods[1]

    w = w_in_mix[0]
    offs = np.cumsum((0,) + MIX_SPLITS)
    w_qa, w_kva, w_kpe, w_ga, w_z, w_xbc, w_dt = [w[:, offs[i]:offs[i + 1]] for i in range(len(MIX_SPLITS))]
    half = MLA_ROPE // 2
    kpe_pad = jnp.zeros((d, LANES), F32).at[:, MLA_NOPE:MLA_NOPE + MLA_ROPE].set(w_kpe)
    kpe_swap = jnp.concatenate([w_kpe[:, half:], w_kpe[:, :half]], axis=1)
    kps_pad = jnp.zeros((d, LANES), F32).at[:, MLA_NOPE:MLA_NOPE + MLA_ROPE].set(kpe_swap)
    dt_pad = jnp.pad(jnp.tile(w_dt, (1, DT_REPL)), ((0, 0), (0, LANES - DT_REPL * DT_COLS)))
    wn = jnp.concatenate([w_qa, w_kva, kpe_pad, kps_pad, w_z, w_xbc, dt_pad], axis=1).astype(BF16)
    wg_t = w_ga.T.astype(BF16)
    wuq = w_uq[0].reshape(MLA_Q_RANK, MLA_HEADS, MLA_NOPE + MLA_ROPE)
    wuq = jnp.pad(wuq, ((0, 0), (0, 0), (0, HEAD_PAD - MLA_NOPE - MLA_ROPE)))
    wuq_t = wuq.reshape(MLA_Q_RANK, MLA_HEADS * HEAD_PAD).T.astype(BF16)
    wukv = w_ukv[0].reshape(MLA_KV_RANK, MLA_HEADS, MLA_NOPE + MLA_V)
    wuk_p = jnp.pad(wukv[:, :, :MLA_NOPE], ((0, 0), (0, 0), (0, HEAD_PAD - MLA_NOPE)))
    wuk_p = wuk_p.reshape(MLA_KV_RANK, MLA_HEADS * HEAD_PAD).astype(BF16)
    wuv_t = wukv[:, :, MLA_NOPE:].reshape(MLA_KV_RANK, MLA_HEADS * MLA_V).T.astype(BF16)
    cos_q, sin_q, cos_k, sin_k = _rope_tables(n, nctx)

    q_t, kcat, v_t, g_t, sz, xbc, dt_raw = _mix_in(
        x, ctx, mod0, norm_w[0:1], wn, wg_t, q_norm[0:1], kv_norm[0:1], wuq_t, wuk_p, wuv_t,
        cos_q, sin_q, cos_k, sin_k, tm)

    a_t = _attention(q_t, kcat, v_t, g_t, tq=512)

    yf, yb = _ssd(xbc, dt_raw, conv_w[0], conv_b[0:1], _pad_row(jnp.tile(a_log[0].reshape(-1), DT_REPL)),
                  _pad_row(jnp.tile(dt_bias[0].reshape(-1), DT_REPL)),
                  jnp.repeat(d_skip[0].astype(F32), SSD_HEAD_DIM, axis=1), nctx)

    wo = w_out_mix[0].astype(BF16)
    x1, u, sg = _mix_out(a_t, yf, yb, sz, x, mod0, mod1, ssd_norm[0:1], wo[:MLA_WIDTH], wo[MLA_WIDTH:],
                         norm_w[1:2], w_in_pool[0].astype(BF16), tm)

    return _pool_out(u, sg, x1, mod1, pool_lin[0].astype(BF16), pool_scale[0:1], w_out_pool[0].astype(BF16),
                     final_norm.reshape(1, d), tm)
```

```python
import functools
import math

import jax
import jax.numpy as jnp
import numpy as np
from jax import lax
from jax.experimental import pallas as pl
from jax.experimental.pallas import tpu as pltpu

D_MODEL = 1024
GRID_W = 64

MLA_HEADS = 8
MLA_NOPE = 64
MLA_ROPE = 32
MLA_V = 64
MLA_Q_RANK = 384
MLA_KV_RANK = 256
MLA_WIDTH = MLA_HEADS * MLA_V
ROPE_BASE = 10000.0
HEAD_PAD = 128
V_ROWS = 80

SSD_HEADS = 8
SSD_HEAD_DIM = 64
SSD_WIDTH = SSD_HEADS * SSD_HEAD_DIM
SSD_GROUPS = 2
SSD_STATE = 64
SSD_CONV = 3
SSD_CHUNK = 128
SSD_CONV_DIM = SSD_WIDTH + 2 * SSD_GROUPS * SSD_STATE
MIX_SPLITS = (MLA_Q_RANK, MLA_KV_RANK, MLA_ROPE, MLA_WIDTH, SSD_WIDTH, SSD_CONV_DIM, 2 * SSD_HEADS)
DT_COLS = 2 * SSD_HEADS
DT_REPL = 3

POOL_WINDOWS = (2, 4, 8, 16)
POOL_GROUPS = len(POOL_WINDOWS)
POOL_WIDTH = D_MODEL
POOL_GROUP_DIM = POOL_WIDTH // POOL_GROUPS
POOL_HALO = 8

RMS_EPS = 1e-6
LANES = 128
SUBLANES = 8
VMEM_LIMIT = 56 * 1024 * 1024

C_QA = 0
C_KVA = C_QA + MLA_Q_RANK
C_KPE = C_KVA + MLA_KV_RANK
C_KPS = C_KPE + LANES
C_Z = C_KPS + LANES
C_XBC = C_Z + SSD_WIDTH
C_DT = C_XBC + SSD_CONV_DIM
C_END = C_DT + LANES

BF16 = jnp.bfloat16
F32 = jnp.float32
NEG_BIG = -1e30


def _rms(x, w):
    y = x * lax.rsqrt(jnp.mean(x * x, axis=-1, keepdims=True) + RMS_EPS)
    return y * w


def _silu(x):
    return x * (1.0 / (1.0 + jnp.exp(-x)))


def _softplus(x):
    return jnp.maximum(x, 0.0) + jnp.log(1.0 + jnp.exp(-jnp.abs(x)))


def _dot(a, b):
    return jnp.dot(a, b, preferred_element_type=F32)


def _dot_nt(a, b):
    return lax.dot_general(a, b, (((1,), (1,)), ((), ())), preferred_element_type=F32)


def _dot_tn(a, b):
    return lax.dot_general(a, b, (((0,), (0,)), ((), ())), preferred_element_type=F32)


def _params(sem):
    return pltpu.CompilerParams(dimension_semantics=sem, vmem_limit_bytes=VMEM_LIMIT)


def _mod_kernel(c_ref, w_ref, b_ref, o_ref):
    sc = _silu(c_ref[...]).astype(BF16)
    o_ref[0] = _dot(sc, w_ref[0].astype(BF16)) + b_ref[0]


def _modulation(cvec, mod_w, mod_b):
    depth, d, d3 = mod_w.shape
    rows = cvec.shape[0]
    nblk = d3 // d
    return pl.pallas_call(
        _mod_kernel,
        out_shape=jax.ShapeDtypeStruct((depth, rows, d3), F32),
        grid=(depth, nblk),
        in_specs=[
            pl.BlockSpec((rows, d), lambda i, j: (0, 0)),
            pl.BlockSpec((1, d, d), lambda i, j: (i, 0, j)),
            pl.BlockSpec((1, 1, d), lambda i, j: (i, 0, j)),
        ],
        out_specs=pl.BlockSpec((1, rows, d), lambda i, j: (i, 0, j)),
        compiler_params=_params(("arbitrary", "arbitrary")),
        name="modulation",
    )(cvec, mod_w, mod_b.reshape(depth, 1, d3))


def _mix_in_kernel(nct, nbatch, x_ref, ctx_ref, mod_ref, nw_ref, wn_ref, wg_ref, qn_ref, kvn_ref,
                   wuq_ref, wuk_ref, wuv_ref, cosq_ref, sinq_ref, cosk_ref, sink_ref,
                   q_ref, k_ref, v_ref, g_ref, sz_ref, xbc_ref, dt_ref):
    b = pl.program_id(0)
    t = pl.program_id(1)
    is_ctx = t < nct
    xin = jnp.where(is_ctx, ctx_ref[0], x_ref[0])
    row = jnp.where(is_ctx, nbatch, b)
    mod = mod_ref[pl.ds(row, 1), :]
    shift = mod[:, 0:D_MODEL]
    scale = mod[:, D_MODEL:2 * D_MODEL]
    h = _rms(xin, nw_ref[...]) * (1.0 + scale) + shift
    hb = h.astype(BF16)
    proj = _dot(hb, wn_ref[...])

    xbc_ref[0] = proj[:, C_XBC:C_DT]
    dt_ref[0] = proj[:, C_DT:C_END]

    ckv = _rms(proj[:, C_KVA:C_KPE], kvn_ref[...]).astype(BF16)
    kpad = _dot(ckv, wuk_ref[...])
    kpe = proj[:, C_KPE:C_KPS] * cosk_ref[...] + proj[:, C_KPS:C_Z] * sink_ref[...]
    vt = _dot_nt(wuv_ref[...], ckv)
    tm = vt.shape[1]
    ones_rows = (lax.broadcasted_iota(jnp.int32, (V_ROWS - MLA_V, tm), 0) == 0).astype(BF16)
    for hd in range(MLA_HEADS):
        k_ref[0, hd] = (kpad[:, hd * HEAD_PAD:(hd + 1) * HEAD_PAD] + kpe).astype(BF16)
        v_ref[0, hd, 0, 0:MLA_V, :] = vt[hd * MLA_V:(hd + 1) * MLA_V].astype(BF16)
        v_ref[0, hd, 0, MLA_V:V_ROWS, :] = ones_rows

    @pl.when(jnp.logical_not(is_ctx))
    def _():
        sz_ref[0] = _silu(proj[:, C_Z:C_XBC])
        g_ref[0] = _silu(_dot_nt(wg_ref[...], hb))
        cq = _rms(proj[:, C_QA:C_KVA], qn_ref[...]).astype(BF16)
        qt = _dot_nt(wuq_ref[...], cq)
        sm_scale = (MLA_NOPE + MLA_ROPE) ** -0.5 * math.log2(math.e)
        cos = cosq_ref[...]
        sin = sinq_ref[...]
        half = MLA_ROPE // 2
        for hd in range(MLA_HEADS):
            base = hd * HEAD_PAD
            t1 = qt[base + MLA_NOPE:base + MLA_NOPE + half]
            t2 = qt[base + MLA_NOPE + half:base + MLA_NOPE + MLA_ROPE]
            q_ref[0, hd, 0, 0:MLA_NOPE, :] = (qt[base:base + MLA_NOPE] * sm_scale).astype(BF16)
            q_ref[0, hd, 0, MLA_NOPE:MLA_NOPE + half, :] = ((t1 * cos - t2 * sin) * sm_scale).astype(BF16)
            q_ref[0, hd, 0, MLA_NOPE + half:MLA_NOPE + MLA_ROPE, :] = ((t2 * cos + t1 * sin) * sm_scale).astype(BF16)
            q_ref[0, hd, 0, MLA_NOPE + MLA_ROPE:HEAD_PAD, :] = jnp.zeros((HEAD_PAD - MLA_NOPE - MLA_ROPE, tm), BF16)


def _mix_in(x, ctx, mod0, norm_w, wn, wg_t, q_norm, kv_norm, wuq_t, wuk_p, wuv_t,
            cos_q, sin_q, cos_k, sin_k, tm):
    bsz, n, d = x.shape
    nctx = ctx.shape[1]
    nct = nctx // tm
    nlt = n // tm
    tt = nctx + n
    rows = mod0.shape[0]

    def lat(t):
        return jnp.maximum(t - nct, 0)

    const = lambda b, t: (0, 0)
    out_shape = (
        jax.ShapeDtypeStruct((bsz, MLA_HEADS, n // tm, HEAD_PAD, tm), BF16),
        jax.ShapeDtypeStruct((bsz, MLA_HEADS, tt, HEAD_PAD), BF16),
        jax.ShapeDtypeStruct((bsz, MLA_HEADS, tt // tm, V_ROWS, tm), BF16),
        jax.ShapeDtypeStruct((bsz, MLA_WIDTH, n), F32),
        jax.ShapeDtypeStruct((bsz, n, SSD_WIDTH), F32),
        jax.ShapeDtypeStruct((bsz, tt, SSD_CONV_DIM), F32),
        jax.ShapeDtypeStruct((bsz, tt, LANES), F32),
    )
    out_specs = (
        pl.BlockSpec((1, MLA_HEADS, 1, HEAD_PAD, tm), lambda b, t: (b, 0, lat(t), 0, 0)),
        pl.BlockSpec((1, MLA_HEADS, tm, HEAD_PAD), lambda b, t: (b, 0, t, 0)),
        pl.BlockSpec((1, MLA_HEADS, 1, V_ROWS, tm), lambda b, t: (b, 0, t, 0, 0)),
        pl.BlockSpec((1, MLA_WIDTH, tm), lambda b, t: (b, 0, lat(t))),
        pl.BlockSpec((1, tm, SSD_WIDTH), lambda b, t: (b, lat(t), 0)),
        pl.BlockSpec((1, tm, SSD_CONV_DIM), lambda b, t: (b, t, 0)),
        pl.BlockSpec((1, tm, LANES), lambda b, t: (b, t, 0)),
    )
    in_specs = [
        pl.BlockSpec((1, tm, d), lambda b, t: (b, lat(t), 0)),
        pl.BlockSpec((1, tm, d), lambda b, t: (b, jnp.minimum(t, nct - 1), 0)),
        pl.BlockSpec((rows, 3 * d), const),
        pl.BlockSpec((1, d), const),
        pl.BlockSpec(wn.shape, const),
        pl.BlockSpec(wg_t.shape, const),
        pl.BlockSpec((1, MLA_Q_RANK), const),
        pl.BlockSpec((1, MLA_KV_RANK), const),
        pl.BlockSpec(wuq_t.shape, const),
        pl.BlockSpec(wuk_p.shape, const),
        pl.BlockSpec(wuv_t.shape, const),
        pl.BlockSpec((MLA_ROPE // 2, tm), lambda b, t: (0, lat(t))),
        pl.BlockSpec((MLA_ROPE // 2, tm), lambda b, t: (0, lat(t))),
        pl.BlockSpec((tm, LANES), lambda b, t: (t, 0)),
        pl.BlockSpec((tm, LANES), lambda b, t: (t, 0)),
    ]
    return pl.pallas_call(
        functools.partial(_mix_in_kernel, nct, bsz),
        out_shape=out_shape,
        grid=(bsz, nct + nlt),
        in_specs=in_specs,
        out_specs=out_specs,
        compiler_params=_params(("arbitrary", "arbitrary")),
        name="mix_in",
    )(x, ctx, mod0, norm_w, wn, wg_t, q_norm, kv_norm, wuq_t, wuk_p, wuv_t, cos_q, sin_q, cos_k, sin_k)


def _attn_kernel(nkb, sub, kt, ntiles, q_ref, k_ref, v_ref, g_ref, o_ref, sa_ref, sb_ref, ma_ref):
    i = pl.program_id(2)
    tq = 2 * q_ref.shape[4]
    kb = sub * kt

    def q_tile(t):
        return jnp.concatenate([q_ref[0, 0, 2 * t], q_ref[0, 0, 2 * t + 1]], axis=1)

    def scores(j, q, s_ref):
        s = _dot(k_ref[0, 0, j * kb:(j + 1) * kb, :], q)
        s_ref[j] = s
        while s.shape[0] > SUBLANES and s.shape[0] % (2 * SUBLANES) == 0:
            half = s.shape[0] // 2
            s = jnp.maximum(s[:half], s[half:])
        return jnp.max(s, axis=0, keepdims=True)

    def weighted_values(j, s_ref, m, acc):
        p = jnp.exp2(s_ref[j] - m).astype(BF16)
        for r in range(sub):
            acc = acc + _dot(v_ref[0, 0, j * sub + r], p[r * kt:(r + 1) * kt])
        return acc

    def fused(q_next, w_ref, r_ref, m_read):
        acc = jnp.zeros((V_ROWS, tq), F32)
        m_next = jnp.full((1, tq), NEG_BIG, F32)
        for j in range(nkb):
            m_next = jnp.maximum(m_next, scores(j, q_next, w_ref))
            acc = weighted_values(j, r_ref, m_read, acc)
        return acc, m_next

    def finish(acc, lo):
        out = acc[0:MLA_V] / acc[MLA_V:MLA_V + 1]
        o_ref[0, :, lo:lo + tq] = (out * g_ref[0, :, lo:lo + tq]).astype(BF16)

    @pl.when(i == 0)
    def _():
        q0 = q_tile(0)
        m0 = jnp.full((1, tq), NEG_BIG, F32)
        for j in range(nkb):
            m0 = jnp.maximum(m0, scores(j, q0, sa_ref))
        ma_ref[0:1, :] = m0

    acc, m_b = fused(q_tile(2 * i + 1), sb_ref, sa_ref, ma_ref[0:1, :])
    finish(acc, 0)
    acc, m_a = fused(q_tile(jnp.minimum(2 * i + 2, ntiles - 1)), sa_ref, sb_ref, m_b)
    finish(acc, tq)
    ma_ref[0:1, :] = m_a


def _attention(q_t, kcat, v_t, g_t):
    bsz, nh, nqt, _, qt = q_t.shape
    tq = 2 * qt
    n = nqt * qt
    ntiles = n // tq
    tt = kcat.shape[2]
    nvb, kt = v_t.shape[2], v_t.shape[4]
    sub = 3 if nvb % 3 == 0 else 1
    nkb = nvb // sub
    assert ntiles % 2 == 0
    return pl.pallas_call(
        functools.partial(_attn_kernel, nkb, sub, kt, ntiles),
        out_shape=jax.ShapeDtypeStruct((bsz, nh * MLA_V, n), BF16),
        grid=(bsz, nh, ntiles // 2),
        in_specs=[
            pl.BlockSpec((1, 1, nqt, HEAD_PAD, qt), lambda b, h, i: (b, h, 0, 0, 0)),
            pl.BlockSpec((1, 1, tt, HEAD_PAD), lambda b, h, i: (b, h, 0, 0)),
            pl.BlockSpec((1, 1, nvb, V_ROWS, kt), lambda b, h, i: (b, h, 0, 0, 0)),
            pl.BlockSpec((1, MLA_V, 2 * tq), lambda b, h, i: (b, h, i)),
        ],
        out_specs=pl.BlockSpec((1, MLA_V, 2 * tq), lambda b, h, i: (b, h, i)),
        scratch_shapes=[pltpu.VMEM((nkb, sub * kt, tq), F32), pltpu.VMEM((nkb, sub * kt, tq), F32),
                        pltpu.VMEM((SUBLANES, tq), F32)],
        compiler_params=_params(("arbitrary", "arbitrary", "arbitrary")),
        name="attention",
    )(q_t, kcat, v_t, g_t)


def _ssd_direction(d, first_seg, last_seg, xc_ref, xp_ref, xn_ref, dtm, acs, cw_ref, cb_ref,
                   skip_ref, eh_ref, ew_ref, state_ref, y_ref):
    L = SSD_CHUNK
    xc = xc_ref[0]
    ridx = lax.broadcasted_iota(jnp.int32, (L, 1), 0)
    prev_row = jnp.where(first_seg, 0.0, xp_ref[0, SUBLANES - 1:SUBLANES, :])
    next_row = jnp.where(last_seg, 0.0, xn_ref[0, 0:1, :])
    x_m1 = jnp.where(ridx == 0, prev_row, pltpu.roll(xc, 1, axis=0))
    x_p1 = jnp.where(ridx == L - 1, next_row, pltpu.roll(xc, L - 1, axis=0))
    cw = cw_ref[...]
    conv = cb_ref[...] + x_m1 * cw[0:1] + xc * cw[1:2] + x_p1 * cw[2:3]
    u = _silu(conv)
    xs = u[:, 0:SSD_WIDTH]
    gn = SSD_GROUPS * SSD_STATE
    bm = u[:, SSD_WIDTH:SSD_WIDTH + gn].astype(BF16)
    cm = u[:, SSD_WIDTH + gn:SSD_WIDTH + 2 * gn].astype(BF16)

    e_head = eh_ref[d]
    e_wide = ew_ref[d]
    dt_x = _dot(_split3(dtm), e_head)
    acs3 = _split3(acs)
    acs_x = _dot(acs3, e_head)
    acs_w = _dot(acs3, e_wide)
    acs_t = acs.T
    end = L - 1 if d == 0 else 0
    a_end = acs_x[end:end + 1, :]

    li = lax.broadcasted_iota(jnp.int32, (L, L), 0)
    si = lax.broadcasted_iota(jnp.int32, (L, L), 1)
    mask = (si <= li) if d == 0 else (si >= li)
    lane = lax.broadcasted_iota(jnp.int32, (L, LANES), 1)
    low = lane < SSD_STATE
    zero_b = jnp.zeros((L, LANES), BF16)

    xd = xs * dt_x
    xdb = xd.astype(BF16)
    heads_per_group = SSD_HEADS // SSD_GROUPS
    ys = []
    for g in range(SSD_GROUPS):
        cg = jnp.where(low, cm, zero_b) if g == 0 else jnp.where(low, zero_b, cm)
        cbm = _dot_nt(cg, bm)
        for pr in range(heads_per_group // 2):
            ms = []
            for r in range(2):
                hd = g * heads_per_group + 2 * pr + r
                col = d * SSD_HEADS + hd
                diff = acs_w[:, hd * L:(hd + 1) * L] - acs_t[col:col + 1, :]
                lmat = jnp.exp(jnp.where(mask, diff, -jnp.inf))
                ms.append((cbm * lmat).astype(BF16))
            pair = (g * heads_per_group) // 2 + pr
            xp = xdb[:, pair * LANES:(pair + 1) * LANES]
            rhs = jnp.concatenate([jnp.where(low, xp, zero_b), jnp.where(low, zero_b, xp)], axis=0)
            ys.append(_dot(jnp.concatenate(ms, axis=1), rhs))
    h_in = state_ref[d]
    y = jnp.concatenate(ys, axis=1) + jnp.exp(acs_x) * _dot(cm, h_in.astype(BF16)) + skip_ref[d:d + 1, :] * xs
    y_ref[0] = y

    xdw = (xd * jnp.exp(a_end - acs_x)).astype(BF16)
    st = _dot_tn(bm, xdw)
    gi = lax.broadcasted_iota(jnp.int32, st.shape, 0) // SSD_STATE
    hi = lax.broadcasted_iota(jnp.int32, st.shape, 1) // (heads_per_group * SSD_HEAD_DIM)
    state_ref[d] = h_in * jnp.exp(a_end) + jnp.where(gi == hi, st, 0.0)


def _split3(x):
    lane = lax.broadcasted_iota(jnp.int32, x.shape, 1)
    r1 = x - x.astype(BF16).astype(F32)
    r2 = r1 - r1.astype(BF16).astype(F32)
    return jnp.where(lane < DT_COLS, x, jnp.where(lane < 2 * DT_COLS, r1, r2)).astype(BF16)


def _ssd_kernel(ncc, nchunks, xf_ref, xfp_ref, xfn_ref, dtf_ref, xb_ref, xbp_ref, xbn_ref, dtb_raw_ref,
                cw_ref, cb_ref, alog_ref, dtbias_ref, skip_ref, tri_ref, eh_ref, ew_ref,
                yf_ref, yb_ref, state_ref):
    s = pl.program_id(1)
    L = SSD_CHUNK

    @pl.when(s == 0)
    def _():
        state_ref[...] = jnp.zeros_like(state_ref)

    cf = s
    cbk = jnp.where(s < ncc, ncc - 1 - s, nchunks - 1 - (s - ncc))
    first = lambda c: jnp.logical_or(c == 0, c == ncc)
    last = lambda c: jnp.logical_or(c == ncc - 1, c == nchunks - 1)

    neg_a = -jnp.exp(alog_ref[...])
    dtm_f = _softplus(dtf_ref[0] + dtbias_ref[...])
    dtm_b = _softplus(dtb_raw_ref[0] + dtbias_ref[...])
    a_f = dtm_f * neg_a
    a_b = dtm_b * neg_a
    cs = jnp.dot(tri_ref[...], jnp.concatenate([a_f, a_b], axis=1), preferred_element_type=F32,
                 precision=lax.Precision.HIGHEST)
    acs_f = cs[:, 0:LANES]
    cs_b = cs[:, LANES:2 * LANES]
    acs_b = cs_b[L - 1:L, :] - cs_b + a_b

    _ssd_direction(0, first(cf), last(cf), xf_ref, xfp_ref, xfn_ref, dtm_f, acs_f, cw_ref, cb_ref,
                   skip_ref, eh_ref, ew_ref, state_ref, yf_ref)
    _ssd_direction(1, first(cbk), last(cbk), xb_ref, xbp_ref, xbn_ref, dtm_b, acs_b, cw_ref, cb_ref,
                   skip_ref, eh_ref, ew_ref, state_ref, yb_ref)


def _ssd_constants():
    L = SSD_CHUNK
    tri = np.tril(np.ones((L, L), np.float32))
    e_head = np.zeros((2, LANES, SSD_WIDTH), np.float32)
    e_wide = np.zeros((2, LANES, SSD_HEADS * L), np.float32)
    for d in range(2):
        for rep in range(DT_REPL):
            for h in range(SSD_HEADS):
                r = rep * DT_COLS + d * SSD_HEADS + h
                e_head[d, r, h * SSD_HEAD_DIM:(h + 1) * SSD_HEAD_DIM] = 1.0
                e_wide[d, r, h * L:(h + 1) * L] = 1.0
    return jnp.asarray(tri), jnp.asarray(e_head, BF16), jnp.asarray(e_wide, BF16)


def _ssd(xbc, dt_raw, conv_w, conv_b, alog_row, dtbias_row, skip_x, nctx):
    bsz, tt, cd = xbc.shape
    L = SSD_CHUNK
    tri, e_head, e_wide = _ssd_constants()
    nchunks = tt // L
    ncc = nctx // L
    n = tt - nctx
    r8 = L // SUBLANES
    nrow8 = tt // SUBLANES

    def cf(s):
        return s

    def cb(s):
        return jnp.where(s < ncc, ncc - 1 - s, nchunks - 1 - (s - ncc))

    def specs(cfn):
        return [
            pl.BlockSpec((1, L, cd), lambda b, s: (b, cfn(s), 0)),
            pl.BlockSpec((1, SUBLANES, cd), lambda b, s: (b, jnp.maximum(cfn(s) * r8 - 1, 0), 0)),
            pl.BlockSpec((1, SUBLANES, cd), lambda b, s: (b, jnp.minimum((cfn(s) + 1) * r8, nrow8 - 1), 0)),
            pl.BlockSpec((1, L, LANES), lambda b, s: (b, cfn(s), 0)),
        ]

    const = lambda b, s: (0, 0)
    in_specs = specs(cf) + specs(cb) + [
        pl.BlockSpec((SSD_CONV, cd), const),
        pl.BlockSpec((1, cd), const),
        pl.BlockSpec((1, LANES), const),
        pl.BlockSpec((1, LANES), const),
        pl.BlockSpec(skip_x.shape, const),
        pl.BlockSpec(tri.shape, const),
        pl.BlockSpec(e_head.shape, lambda b, s: (0, 0, 0)),
        pl.BlockSpec(e_wide.shape, lambda b, s: (0, 0, 0)),
    ]
    out_specs = (
        pl.BlockSpec((1, L, SSD_WIDTH), lambda b, s: (b, jnp.maximum(cf(s) - ncc, 0), 0)),
        pl.BlockSpec((1, L, SSD_WIDTH), lambda b, s: (b, nchunks - 1 - ncc - jnp.maximum(s - ncc, 0), 0)),
    )
    y_shape = jax.ShapeDtypeStruct((bsz, n, SSD_WIDTH), F32)
    return pl.pallas_call(
        functools.partial(_ssd_kernel, ncc, nchunks),
        out_shape=(y_shape, y_shape),
        grid=(bsz, nchunks),
        in_specs=in_specs,
        out_specs=out_specs,
        scratch_shapes=[pltpu.VMEM((2, SSD_GROUPS * SSD_STATE, SSD_WIDTH), F32)],
        compiler_params=_params(("arbitrary", "arbitrary")),
        name="ssd",
    )(xbc, xbc, xbc, dt_raw, xbc, xbc, xbc, dt_raw, conv_w, conv_b, alog_row, dtbias_row, skip_x,
      tri, e_head, e_wide)


def _mix_out_kernel(nbatch, a_ref, yf_ref, yb_ref, sz_ref, x_ref, mod0_ref, mod1_ref, sn_ref, woa_ref, wos_ref,
                    nw1_ref, wp_ref, x1_ref, u_ref, sg_ref):
    b = pl.program_id(0)
    y = (yf_ref[0] + yb_ref[0]) * sz_ref[0]
    ssd = _rms(y, sn_ref[...]).astype(BF16)
    o = _dot_tn(a_ref[0], woa_ref[...]) + _dot(ssd, wos_ref[...])
    gate0 = mod0_ref[pl.ds(b, 1), 2 * D_MODEL:3 * D_MODEL]
    x1 = x_ref[0] + gate0 * o
    x1_ref[0] = x1
    mod1 = mod1_ref[pl.ds(b, 1), :]
    h1 = _rms(x1, nw1_ref[...]) * (1.0 + mod1[:, D_MODEL:2 * D_MODEL]) + mod1[:, 0:D_MODEL]
    ug = _dot(h1.astype(BF16), wp_ref[...])
    u_ref[0] = ug[:, 0:POOL_WIDTH]
    sg_ref[0] = _silu(ug[:, POOL_WIDTH:2 * POOL_WIDTH])


def _mix_out(a_t, yf, yb, sz, x, mod0, mod1, ssd_norm, wo_a, wo_s, norm_w1, w_pool, tm):
    bsz, n, d = x.shape
    rows = mod0.shape[0]
    const = lambda b, t: (0, 0)
    tile = lambda w: pl.BlockSpec((1, tm, w), lambda b, t: (b, t, 0))
    return pl.pallas_call(
        functools.partial(_mix_out_kernel, bsz),
        out_shape=(jax.ShapeDtypeStruct((bsz, n, d), F32),
                   jax.ShapeDtypeStruct((bsz, n, POOL_WIDTH), F32),
                   jax.ShapeDtypeStruct((bsz, n, POOL_WIDTH), F32)),
        grid=(bsz, n // tm),
        in_specs=[
            pl.BlockSpec((1, MLA_WIDTH, tm), lambda b, t: (b, 0, t)),
            tile(SSD_WIDTH), tile(SSD_WIDTH), tile(SSD_WIDTH), tile(d),
            pl.BlockSpec((rows, 3 * d), const),
            pl.BlockSpec((rows, 3 * d), const),
            pl.BlockSpec((1, SSD_WIDTH), const),
            pl.BlockSpec(wo_a.shape, const),
            pl.BlockSpec(wo_s.shape, const),
            pl.BlockSpec((1, d), const),
            pl.BlockSpec(w_pool.shape, const),
        ],
        out_specs=(tile(d), tile(POOL_WIDTH), tile(POOL_WIDTH)),
        compiler_params=_params(("arbitrary", "arbitrary")),
        name="mix_out",
    )(a_t, yf, yb, sz, x, mod0, mod1, ssd_norm, wo_a, wo_s, norm_w1, w_pool)


def _pool_kernel(n, u_ref, up_ref, un_ref, sg_ref, x1_ref, mod1_ref, lin_ref, ps_ref, wo_ref, fn_ref,
                 o_ref, ext_ref):
    b = pl.program_id(0)
    t = pl.program_id(1)
    nt = pl.num_programs(1)
    tm = u_ref.shape[1]
    H = POOL_HALO
    ext_ref[0:H, :] = jnp.where(t == 0, 0.0, up_ref[0])
    ext_ref[H:H + tm, :] = u_ref[0]
    ext_ref[H + tm:H + tm + H, :] = jnp.where(t == nt - 1, 0.0, un_ref[0])

    tok = t * tm + lax.broadcasted_iota(jnp.int32, (tm, 1), 0)
    ys = []
    for g, w in enumerate(POOL_WINDOWS):
        lanes = slice(g * POOL_GROUP_DIM, (g + 1) * POOL_GROUP_DIM)
        acc = ext_ref[H - w // 2:H - w // 2 + tm, lanes]
        for j in range(1, w):
            acc = acc + ext_ref[H - w // 2 + j:H - w // 2 + j + tm, lanes]
        lo = jnp.maximum(tok - w // 2, 0)
        hi = jnp.minimum(tok + (w - w // 2 - 1), n - 1)
        cnt = (hi - lo + 1).astype(F32)
        m = acc / cnt - u_ref[0, :, lanes]
        ys.append(_dot(m.astype(BF16), lin_ref[g]))
    y = jnp.concatenate(ys, axis=-1) * ps_ref[...] * sg_ref[0]
    o = _dot(y.astype(BF16), wo_ref[...])
    gate1 = mod1_ref[pl.ds(b, 1), 2 * D_MODEL:3 * D_MODEL]
    x2 = x1_ref[0] + gate1 * o
    o_ref[0] = _rms(x2, fn_ref[...])


def _pool_out(u, sg, x1, mod1, pool_lin, pool_scale, wo_pool, final_norm, tm):
    bsz, n, d = x1.shape
    rows = mod1.shape[0]
    r8 = tm // SUBLANES
    nrow8 = n // SUBLANES
    const = lambda b, t: (0, 0)
    tile = lambda w: pl.BlockSpec((1, tm, w), lambda b, t: (b, t, 0))
    return pl.pallas_call(
        functools.partial(_pool_kernel, n),
        out_shape=jax.ShapeDtypeStruct((bsz, n, d), F32),
        grid=(bsz, n // tm),
        in_specs=[
            tile(POOL_WIDTH),
            pl.BlockSpec((1, POOL_HALO, POOL_WIDTH), lambda b, t: (b, jnp.maximum(t * r8 - 1, 0), 0)),
            pl.BlockSpec((1, POOL_HALO, POOL_WIDTH), lambda b, t: (b, jnp.minimum((t + 1) * r8, nrow8 - 1), 0)),
            tile(POOL_WIDTH), tile(d),
            pl.BlockSpec((rows, 3 * d), const),
            pl.BlockSpec(pool_lin.shape, lambda b, t: (0, 0, 0)),
            pl.BlockSpec((1, POOL_WIDTH), const),
            pl.BlockSpec(wo_pool.shape, const),
            pl.BlockSpec((1, d), const),
        ],
        out_specs=tile(d),
        scratch_shapes=[pltpu.VMEM((tm + 2 * POOL_HALO, POOL_WIDTH), F32)],
        compiler_params=_params(("arbitrary", "arbitrary")),
        name="pool_out",
    )(u, u, u, sg, x1, mod1, pool_lin, pool_scale, wo_pool, final_norm)


def _rope_tables(n, nctx):
    rows = n // GRID_W
    row = jnp.repeat(jnp.arange(rows, dtype=F32), GRID_W)
    col = jnp.tile(jnp.arange(GRID_W, dtype=F32), rows)
    axis_dim = MLA_ROPE // 2
    inv_freq = 1.0 / (ROPE_BASE ** (jnp.arange(0, axis_dim, 2, dtype=F32) / axis_dim))
    ang = jnp.concatenate([row[:, None] * inv_freq, col[:, None] * inv_freq], axis=-1)
    cos, sin = jnp.cos(ang), jnp.sin(ang)
    half = MLA_ROPE // 2
    cos_all = jnp.concatenate([jnp.ones((nctx, half), F32), cos], axis=0)
    sin_all = jnp.concatenate([jnp.zeros((nctx, half), F32), sin], axis=0)
    left = jnp.zeros((nctx + n, MLA_NOPE), F32)
    right = jnp.zeros((nctx + n, LANES - MLA_NOPE - MLA_ROPE), F32)
    cos_k = jnp.concatenate([left, cos_all, cos_all, right], axis=1)
    sin_k = jnp.concatenate([left, -sin_all, sin_all, right], axis=1)
    return cos.T, sin.T, cos_k, sin_k


def _pad_row(v):
    v = v.reshape(1, -1).astype(F32)
    return jnp.pad(v, ((0, 0), (0, LANES - v.shape[1])))


def kernel(x, c, ctx, c_ctx, mod_w, mod_b, norm_w, w_in_mix, q_norm, w_uq, kv_norm, w_ukv, conv_w, conv_b,
           a_log, dt_bias, d_skip, ssd_norm, w_out_mix, w_in_pool, pool_lin, pool_scale, w_out_pool, final_norm):
    bsz, n, d = x.shape
    nctx = ctx.shape[1]
    tm = 256
    tm_out = 512
    assert d == D_MODEL and n % tm_out == 0 and nctx % tm == 0 and n % GRID_W == 0
    assert mod_w.shape[0] == 2, "two layers: one mixing layer followed by one pooling layer"

    rows = -(-(bsz + 1) // SUBLANES) * SUBLANES
    cvec = jnp.zeros((rows, d), F32).at[:bsz].set(c).at[bsz].set(c_ctx)
    mods = _modulation(cvec, mod_w, mod_b)
    mod0, mod1 = mods[0], mods[1]

    w = w_in_mix[0]
    offs = np.cumsum((0,) + MIX_SPLITS)
    w_qa, w_kva, w_kpe, w_ga, w_z, w_xbc, w_dt = [w[:, offs[i]:offs[i + 1]] for i in range(len(MIX_SPLITS))]
    half = MLA_ROPE // 2
    kpe_pad = jnp.zeros((d, LANES), F32).at[:, MLA_NOPE:MLA_NOPE + MLA_ROPE].set(w_kpe)
    kpe_swap = jnp.concatenate([w_kpe[:, half:], w_kpe[:, :half]], axis=1)
    kps_pad = jnp.zeros((d, LANES), F32).at[:, MLA_NOPE:MLA_NOPE + MLA_ROPE].set(kpe_swap)
    dt_pad = jnp.pad(jnp.tile(w_dt, (1, DT_REPL)), ((0, 0), (0, LANES - DT_REPL * DT_COLS)))
    wn = jnp.concatenate([w_qa, w_kva, kpe_pad, kps_pad, w_z, w_xbc, dt_pad], axis=1).astype(BF16)
    wg_t = w_ga.T.astype(BF16)
    wuq = w_uq[0].reshape(MLA_Q_RANK, MLA_HEADS, MLA_NOPE + MLA_ROPE)
    wuq = jnp.pad(wuq, ((0, 0), (0, 0), (0, HEAD_PAD - MLA_NOPE - MLA_ROPE)))
    wuq_t = wuq.reshape(MLA_Q_RANK, MLA_HEADS * HEAD_PAD).T.astype(BF16)
    wukv = w_ukv[0].reshape(MLA_KV_RANK, MLA_HEADS, MLA_NOPE + MLA_V)
    wuk_p = jnp.pad(wukv[:, :, :MLA_NOPE], ((0, 0), (0, 0), (0, HEAD_PAD - MLA_NOPE)))
    wuk_p = wuk_p.reshape(MLA_KV_RANK, MLA_HEADS * HEAD_PAD).astype(BF16)
    wuv_t = wukv[:, :, MLA_NOPE:].reshape(MLA_KV_RANK, MLA_HEADS * MLA_V).T.astype(BF16)
    cos_q, sin_q, cos_k, sin_k = _rope_tables(n, nctx)

    q_t, kcat, v_t, g_t, sz, xbc, dt_raw = _mix_in(
        x, ctx, mod0, norm_w[0:1], wn, wg_t, q_norm[0:1], kv_norm[0:1], wuq_t, wuk_p, wuv_t,
        cos_q, sin_q, cos_k, sin_k, tm)

    a_t = _attention(q_t, kcat, v_t, g_t)

    yf, yb = _ssd(xbc, dt_raw, conv_w[0], conv_b[0:1], _pad_row(jnp.tile(a_log[0].reshape(-1), DT_REPL)),
                  _pad_row(jnp.tile(dt_bias[0].reshape(-1), DT_REPL)),
                  jnp.repeat(d_skip[0].astype(F32), SSD_HEAD_DIM, axis=1), nctx)

    wo = w_out_mix[0].astype(BF16)
    x1, u, sg = _mix_out(a_t, yf, yb, sz, x, mod0, mod1, ssd_norm[0:1], wo[:MLA_WIDTH], wo[MLA_WIDTH:],
                         norm_w[1:2], w_in_pool[0].astype(BF16), tm_out)

    return _pool_out(u, sg, x1, mod1, pool_lin[0].astype(BF16), pool_scale[0:1], w_out_pool[0].astype(BF16),
                     final_norm.reshape(1, d), tm_out)
```

```python
import functools
import math

import jax
import jax.numpy as jnp
import numpy as np
from jax import lax
from jax.experimental import pallas as pl
from jax.experimental.pallas import tpu as pltpu

D_MODEL = 1024
GRID_W = 64

MLA_HEADS = 8
MLA_NOPE = 64
MLA_ROPE = 32
MLA_V = 64
MLA_Q_RANK = 384
MLA_KV_RANK = 256
MLA_WIDTH = MLA_HEADS * MLA_V
ROPE_BASE = 10000.0
HEAD_PAD = 128
V_ROWS = 80

SSD_HEADS = 8
SSD_HEAD_DIM = 64
SSD_WIDTH = SSD_HEADS * SSD_HEAD_DIM
SSD_GROUPS = 2
SSD_STATE = 64
SSD_CONV = 3
SSD_CHUNK = 128
SSD_CONV_DIM = SSD_WIDTH + 2 * SSD_GROUPS * SSD_STATE
MIX_SPLITS = (MLA_Q_RANK, MLA_KV_RANK, MLA_ROPE, MLA_WIDTH, SSD_WIDTH, SSD_CONV_DIM, 2 * SSD_HEADS)
DT_COLS = 2 * SSD_HEADS
DT_REPL = 3

POOL_WINDOWS = (2, 4, 8, 16)
POOL_GROUPS = len(POOL_WINDOWS)
POOL_WIDTH = D_MODEL
POOL_GROUP_DIM = POOL_WIDTH // POOL_GROUPS
POOL_HALO = 8

RMS_EPS = 1e-6
LANES = 128
SUBLANES = 8
VMEM_LIMIT = 56 * 1024 * 1024

C_QA = 0
C_KVA = C_QA + MLA_Q_RANK
C_KPE = C_KVA + MLA_KV_RANK
C_KPS = C_KPE + LANES
C_Z = C_KPS + LANES
C_XBC = C_Z + SSD_WIDTH
C_DT = C_XBC + SSD_CONV_DIM
C_END = C_DT + LANES

BF16 = jnp.bfloat16
F32 = jnp.float32
NEG_BIG = -1e30


def _rms(x, w):
    y = x * lax.rsqrt(jnp.mean(x * x, axis=-1, keepdims=True) + RMS_EPS)
    return y * w


def _silu(x):
    return x * (1.0 / (1.0 + jnp.exp(-x)))


def _softplus(x):
    return jnp.maximum(x, 0.0) + jnp.log(1.0 + jnp.exp(-jnp.abs(x)))


def _dot(a, b):
    return jnp.dot(a, b, preferred_element_type=F32)


def _dot_nt(a, b):
    return lax.dot_general(a, b, (((1,), (1,)), ((), ())), preferred_element_type=F32)


def _dot_tn(a, b):
    return lax.dot_general(a, b, (((0,), (0,)), ((), ())), preferred_element_type=F32)


def _params(sem):
    return pltpu.CompilerParams(dimension_semantics=sem, vmem_limit_bytes=VMEM_LIMIT)


def _mod_kernel(c_ref, w_ref, b_ref, o_ref):
    sc = _silu(c_ref[...]).astype(BF16)
    o_ref[0] = _dot(sc, w_ref[0].astype(BF16)) + b_ref[0]


def _modulation(cvec, mod_w, mod_b):
    depth, d, d3 = mod_w.shape
    rows = cvec.shape[0]
    nblk = d3 // d
    return pl.pallas_call(
        _mod_kernel,
        out_shape=jax.ShapeDtypeStruct((depth, rows, d3), F32),
        grid=(depth, nblk),
        in_specs=[
            pl.BlockSpec((rows, d), lambda i, j: (0, 0)),
            pl.BlockSpec((1, d, d), lambda i, j: (i, 0, j)),
            pl.BlockSpec((1, 1, d), lambda i, j: (i, 0, j)),
        ],
        out_specs=pl.BlockSpec((1, rows, d), lambda i, j: (i, 0, j)),
        compiler_params=_params(("arbitrary", "arbitrary")),
        name="modulation",
    )(cvec, mod_w, mod_b.reshape(depth, 1, d3))


def _mix_in_kernel(nct, nt, nbatch, x_ref, ctx_ref, mod_ref, nw_ref, wn_ref, wg_ref, qn_ref, kvn_ref,
                   wuq_ref, wuk_ref, wuv_ref, cosq_ref, sinq_ref, cosk_ref, sink_ref, cw_ref, cb_ref,
                   q_ref, k_ref, v_ref, g_ref, sz_ref, u_ref, dt_ref, xbuf_ref, last_ref):
    b = pl.program_id(0)
    t = pl.program_id(1)

    @pl.when(t == 0)
    def _():
        last_ref[...] = jnp.zeros_like(last_ref)
        xbuf_ref[...] = jnp.zeros_like(xbuf_ref)

    def finish_conv(xbc_new):
        xc = xbuf_ref[(t + 1) % 2]
        xbuf_ref[t % 2] = xbc_new
        tp = t - 1
        tm = xc.shape[0]
        first_seg = jnp.logical_or(tp == 0, tp == nct)
        last_seg = jnp.logical_or(tp == nct - 1, tp == nt - 1)
        prev_row = jnp.where(first_seg, 0.0, last_ref[0:1, :])
        next_row = jnp.where(last_seg, 0.0, xbc_new[0:1, :])
        ridx = lax.broadcasted_iota(jnp.int32, (tm, 1), 0)
        x_m1 = jnp.where(ridx == 0, prev_row, pltpu.roll(xc, 1, axis=0))
        x_p1 = jnp.where(ridx == tm - 1, next_row, pltpu.roll(xc, tm - 1, axis=0))
        cw = cw_ref[...]
        u_ref[0] = _silu(cb_ref[...] + x_m1 * cw[0:1] + xc * cw[1:2] + x_p1 * cw[2:3])
        last_ref[0:1, :] = xc[tm - 1:tm]

    _mix_in_project(nct, nbatch, b, jnp.minimum(t, nt - 1), x_ref, ctx_ref, mod_ref, nw_ref, wn_ref,
                    wg_ref, qn_ref, kvn_ref, wuq_ref, wuk_ref, wuv_ref, cosq_ref, sinq_ref,
                    cosk_ref, sink_ref, q_ref, k_ref, v_ref, g_ref, sz_ref, dt_ref, finish_conv)


def _mix_in_project(nct, nbatch, b, t, x_ref, ctx_ref, mod_ref, nw_ref, wn_ref, wg_ref, qn_ref, kvn_ref,
                    wuq_ref, wuk_ref, wuv_ref, cosq_ref, sinq_ref, cosk_ref, sink_ref,
                    q_ref, k_ref, v_ref, g_ref, sz_ref, dt_ref, finish_conv):
    is_ctx = t < nct
    xin = jnp.where(is_ctx, ctx_ref[0], x_ref[0])
    row = jnp.where(is_ctx, nbatch, b)
    mod = mod_ref[pl.ds(row, 1), :]
    shift = mod[:, 0:D_MODEL]
    scale = mod[:, D_MODEL:2 * D_MODEL]
    h = _rms(xin, nw_ref[...]) * (1.0 + scale) + shift
    hb = h.astype(BF16)
    proj = _dot(hb, wn_ref[...])

    dt_ref[0] = proj[:, C_DT:C_END]
    finish_conv(proj[:, C_XBC:C_DT])

    ckv = _rms(proj[:, C_KVA:C_KPE], kvn_ref[...]).astype(BF16)
    kpad = _dot(ckv, wuk_ref[...])
    kpe = proj[:, C_KPE:C_KPS] * cosk_ref[...] + proj[:, C_KPS:C_Z] * sink_ref[...]
    vt = _dot_nt(wuv_ref[...], ckv)
    tm = vt.shape[1]
    ones_rows = (lax.broadcasted_iota(jnp.int32, (V_ROWS - MLA_V, tm), 0) == 0).astype(BF16)
    for hd in range(MLA_HEADS):
        k_ref[0, hd] = (kpad[:, hd * HEAD_PAD:(hd + 1) * HEAD_PAD] + kpe).astype(BF16)
        v_ref[0, hd, 0, 0:MLA_V, :] = vt[hd * MLA_V:(hd + 1) * MLA_V].astype(BF16)
        v_ref[0, hd, 0, MLA_V:V_ROWS, :] = ones_rows

    @pl.when(jnp.logical_not(is_ctx))
    def _():
        sz_ref[0] = _silu(proj[:, C_Z:C_XBC])
        g_ref[0] = _silu(_dot_nt(wg_ref[...], hb))
        cq = _rms(proj[:, C_QA:C_KVA], qn_ref[...]).astype(BF16)
        qt = _dot_nt(wuq_ref[...], cq)
        sm_scale = (MLA_NOPE + MLA_ROPE) ** -0.5 * math.log2(math.e)
        cos = cosq_ref[...]
        sin = sinq_ref[...]
        half = MLA_ROPE // 2
        for hd in range(MLA_HEADS):
            base = hd * HEAD_PAD
            t1 = qt[base + MLA_NOPE:base + MLA_NOPE + half]
            t2 = qt[base + MLA_NOPE + half:base + MLA_NOPE + MLA_ROPE]
            q_ref[0, hd, 0, 0:MLA_NOPE, :] = (qt[base:base + MLA_NOPE] * sm_scale).astype(BF16)
            q_ref[0, hd, 0, MLA_NOPE:MLA_NOPE + half, :] = ((t1 * cos - t2 * sin) * sm_scale).astype(BF16)
            q_ref[0, hd, 0, MLA_NOPE + half:MLA_NOPE + MLA_ROPE, :] = ((t2 * cos + t1 * sin) * sm_scale).astype(BF16)
            q_ref[0, hd, 0, MLA_NOPE + MLA_ROPE:HEAD_PAD, :] = jnp.zeros((HEAD_PAD - MLA_NOPE - MLA_ROPE, tm), BF16)


def _mix_in(x, ctx, mod0, norm_w, wn, wg_t, q_norm, kv_norm, wuq_t, wuk_p, wuv_t,
            cos_q, sin_q, cos_k, sin_k, conv_w, conv_b, tm):
    bsz, n, d = x.shape
    nctx = ctx.shape[1]
    nct = nctx // tm
    nlt = n // tm
    nt = nct + nlt
    tt = nctx + n
    rows = mod0.shape[0]

    def lat(t):
        return jnp.clip(t - nct, 0, nlt - 1)

    def cur(t):
        return jnp.minimum(t, nt - 1)

    const = lambda b, t: (0, 0)
    out_shape = (
        jax.ShapeDtypeStruct((bsz, MLA_HEADS, n // tm, HEAD_PAD, tm), BF16),
        jax.ShapeDtypeStruct((bsz, MLA_HEADS, tt, HEAD_PAD), BF16),
        jax.ShapeDtypeStruct((bsz, MLA_HEADS, tt // tm, V_ROWS, tm), BF16),
        jax.ShapeDtypeStruct((bsz, MLA_WIDTH, n), F32),
        jax.ShapeDtypeStruct((bsz, n, SSD_WIDTH), F32),
        jax.ShapeDtypeStruct((bsz, tt, SSD_CONV_DIM), F32),
        jax.ShapeDtypeStruct((bsz, tt, LANES), F32),
    )
    out_specs = (
        pl.BlockSpec((1, MLA_HEADS, 1, HEAD_PAD, tm), lambda b, t: (b, 0, lat(t), 0, 0)),
        pl.BlockSpec((1, MLA_HEADS, tm, HEAD_PAD), lambda b, t: (b, 0, cur(t), 0)),
        pl.BlockSpec((1, MLA_HEADS, 1, V_ROWS, tm), lambda b, t: (b, 0, cur(t), 0, 0)),
        pl.BlockSpec((1, MLA_WIDTH, tm), lambda b, t: (b, 0, lat(t))),
        pl.BlockSpec((1, tm, SSD_WIDTH), lambda b, t: (b, lat(t), 0)),
        pl.BlockSpec((1, tm, SSD_CONV_DIM), lambda b, t: (b, jnp.maximum(t - 1, 0), 0)),
        pl.BlockSpec((1, tm, LANES), lambda b, t: (b, cur(t), 0)),
    )
    in_specs = [
        pl.BlockSpec((1, tm, d), lambda b, t: (b, lat(t), 0)),
        pl.BlockSpec((1, tm, d), lambda b, t: (b, jnp.minimum(t, nct - 1), 0)),
        pl.BlockSpec((rows, 3 * d), const),
        pl.BlockSpec((1, d), const),
        pl.BlockSpec(wn.shape, const),
        pl.BlockSpec(wg_t.shape, const),
        pl.BlockSpec((1, MLA_Q_RANK), const),
        pl.BlockSpec((1, MLA_KV_RANK), const),
        pl.BlockSpec(wuq_t.shape, const),
        pl.BlockSpec(wuk_p.shape, const),
        pl.BlockSpec(wuv_t.shape, const),
        pl.BlockSpec((MLA_ROPE // 2, tm), lambda b, t: (0, lat(t))),
        pl.BlockSpec((MLA_ROPE // 2, tm), lambda b, t: (0, lat(t))),
        pl.BlockSpec((tm, LANES), lambda b, t: (cur(t), 0)),
        pl.BlockSpec((tm, LANES), lambda b, t: (cur(t), 0)),
        pl.BlockSpec(conv_w.shape, const),
        pl.BlockSpec(conv_b.shape, const),
    ]
    return pl.pallas_call(
        functools.partial(_mix_in_kernel, nct, nt, bsz),
        out_shape=out_shape,
        grid=(bsz, nt + 1),
        in_specs=in_specs,
        out_specs=out_specs,
        scratch_shapes=[pltpu.VMEM((2, tm, SSD_CONV_DIM), F32), pltpu.VMEM((SUBLANES, SSD_CONV_DIM), F32)],
        compiler_params=_params(("arbitrary", "arbitrary")),
        name="mix_in",
    )(x, ctx, mod0, norm_w, wn, wg_t, q_norm, kv_norm, wuq_t, wuk_p, wuv_t, cos_q, sin_q, cos_k, sin_k,
      conv_w, conv_b)


def _attn_kernel(nkb, sub, kt, ntiles, q_ref, k_ref, v_ref, g_ref, qn_ref, kn_ref, o_ref, sa_ref, sb_ref, ma_ref):
    i = pl.program_id(2)
    first_step = jnp.logical_and(jnp.logical_and(pl.program_id(0) == 0, pl.program_id(1) == 0), i == 0)
    last_of_head = i == ntiles // 2 - 1
    tq = 2 * q_ref.shape[4]
    kb = sub * kt

    def q_tile(t):
        return jnp.concatenate([q_ref[0, 0, 2 * t], q_ref[0, 0, 2 * t + 1]], axis=1)

    def keys(j, look_ahead):
        rows = slice(j * kb, (j + 1) * kb)
        if look_ahead:
            return jnp.where(last_of_head, kn_ref[0, 0, rows, :], k_ref[0, 0, rows, :])
        return k_ref[0, 0, rows, :]

    def scores(j, q, s_ref, look_ahead=False):
        s = _dot(keys(j, look_ahead), q)
        s_ref[j] = s
        while s.shape[0] > SUBLANES and s.shape[0] % (2 * SUBLANES) == 0:
            half = s.shape[0] // 2
            s = jnp.maximum(s[:half], s[half:])
        return jnp.max(s, axis=0, keepdims=True)

    def weighted_values(j, s_ref, m, acc):
        p = jnp.exp2(s_ref[j] - m).astype(BF16)
        for r in range(sub):
            acc = acc + _dot(v_ref[0, 0, j * sub + r], p[r * kt:(r + 1) * kt])
        return acc

    def fused(q_next, w_ref, r_ref, m_read, look_ahead=False):
        acc = jnp.zeros((V_ROWS, tq), F32)
        m_next = jnp.full((1, tq), NEG_BIG, F32)
        for j in range(nkb):
            m_next = jnp.maximum(m_next, scores(j, q_next, w_ref, look_ahead))
            acc = weighted_values(j, r_ref, m_read, acc)
        return acc, m_next

    def finish(acc, lo):
        out = acc[0:MLA_V] / acc[MLA_V:MLA_V + 1]
        o_ref[0, :, lo:lo + tq] = (out * g_ref[0, :, lo:lo + tq]).astype(BF16)

    @pl.when(first_step)
    def _():
        q0 = q_tile(0)
        m0 = jnp.full((1, tq), NEG_BIG, F32)
        for j in range(nkb):
            m0 = jnp.maximum(m0, scores(j, q0, sa_ref))
        ma_ref[0:1, :] = m0

    acc, m_b = fused(q_tile(2 * i + 1), sb_ref, sa_ref, ma_ref[0:1, :])
    finish(acc, 0)
    q_ahead = jnp.concatenate([qn_ref[0, 0, 0], qn_ref[0, 0, 1]], axis=1)
    q_next = jnp.where(last_of_head, q_ahead, q_tile(jnp.minimum(2 * i + 2, ntiles - 1)))
    acc, m_a = fused(q_next, sa_ref, sb_ref, m_b, look_ahead=True)
    finish(acc, tq)
    ma_ref[0:1, :] = m_a


def _attention(q_t, kcat, v_t, g_t):
    bsz, nh, nqt, _, qt = q_t.shape
    tq = 2 * qt
    n = nqt * qt
    ntiles = n // tq
    tt = kcat.shape[2]
    nvb, kt = v_t.shape[2], v_t.shape[4]
    sub = 3 if nvb % 3 == 0 else 1
    nkb = nvb // sub
    assert ntiles % 2 == 0

    def nxt(b, h):
        f = jnp.minimum(b * nh + h + 1, bsz * nh - 1)
        return f // nh, f % nh

    return pl.pallas_call(
        functools.partial(_attn_kernel, nkb, sub, kt, ntiles),
        out_shape=jax.ShapeDtypeStruct((bsz, nh * MLA_V, n), BF16),
        grid=(bsz, nh, ntiles // 2),
        in_specs=[
            pl.BlockSpec((1, 1, nqt, HEAD_PAD, qt), lambda b, h, i: (b, h, 0, 0, 0)),
            pl.BlockSpec((1, 1, tt, HEAD_PAD), lambda b, h, i: (b, h, 0, 0)),
            pl.BlockSpec((1, 1, nvb, V_ROWS, kt), lambda b, h, i: (b, h, 0, 0, 0)),
            pl.BlockSpec((1, MLA_V, 2 * tq), lambda b, h, i: (b, h, i)),
            pl.BlockSpec((1, 1, 2, HEAD_PAD, qt), lambda b, h, i: (*nxt(b, h), 0, 0, 0)),
            pl.BlockSpec((1, 1, tt, HEAD_PAD), lambda b, h, i: (*nxt(b, h), 0, 0)),
        ],
        out_specs=pl.BlockSpec((1, MLA_V, 2 * tq), lambda b, h, i: (b, h, i)),
        scratch_shapes=[pltpu.VMEM((nkb, sub * kt, tq), F32), pltpu.VMEM((nkb, sub * kt, tq), F32),
                        pltpu.VMEM((SUBLANES, tq), F32)],
        compiler_params=_params(("arbitrary", "arbitrary", "arbitrary")),
        name="attention",
    )(q_t, kcat, v_t, g_t, q_t, kcat)


def _ssd_direction(d, u_ref, dtm, acs, skip_ref, eh_ref, ew_ref, state_ref, y_ref):
    L = SSD_CHUNK
    u = u_ref[0]
    xs = u[:, 0:SSD_WIDTH]
    gn = SSD_GROUPS * SSD_STATE
    bm = u[:, SSD_WIDTH:SSD_WIDTH + gn].astype(BF16)
    cm = u[:, SSD_WIDTH + gn:SSD_WIDTH + 2 * gn].astype(BF16)

    e_head = eh_ref[d]
    e_wide = ew_ref[d]
    dt_x = _dot(_split3(dtm), e_head)
    acs3 = _split3(acs)
    acs_x = _dot(acs3, e_head)
    acs_t = acs.T
    end = L - 1 if d == 0 else 0
    a_end = acs_x[end:end + 1, :]

    li = lax.broadcasted_iota(jnp.int32, (L, L), 0)
    si = lax.broadcasted_iota(jnp.int32, (L, L), 1)
    mask = (si <= li) if d == 0 else (si >= li)
    lane = lax.broadcasted_iota(jnp.int32, (L, LANES), 1)
    low = lane < SSD_STATE
    zero_b = jnp.zeros((L, LANES), BF16)

    xd = xs * dt_x
    xdb = xd.astype(BF16)
    heads_per_group = SSD_HEADS // SSD_GROUPS
    ys = []
    for g in range(SSD_GROUPS):
        cg = jnp.where(low, cm, zero_b) if g == 0 else jnp.where(low, zero_b, cm)
        cbm = _dot_nt(cg, bm)
        for pr in range(heads_per_group // 2):
            ms = []
            h0 = g * heads_per_group + 2 * pr
            acs_w = _dot(acs3, e_wide[:, h0 * L:(h0 + 2) * L])
            for r in range(2):
                hd = h0 + r
                col = d * SSD_HEADS + hd
                diff = acs_w[:, r * L:(r + 1) * L] - acs_t[col:col + 1, :]
                lmat = jnp.exp(jnp.where(mask, diff, -jnp.inf))
                ms.append((cbm * lmat).astype(BF16))
            pair = (g * heads_per_group) // 2 + pr
            xp = xdb[:, pair * LANES:(pair + 1) * LANES]
            rhs = jnp.concatenate([jnp.where(low, xp, zero_b), jnp.where(low, zero_b, xp)], axis=0)
            ys.append(_dot(jnp.concatenate(ms, axis=1), rhs))
    h_in = state_ref[d]
    y = jnp.concatenate(ys, axis=1) + jnp.exp(acs_x) * _dot(cm, h_in.astype(BF16)) + skip_ref[d:d + 1, :] * xs
    y_ref[0] = y

    xdw = (xd * jnp.exp(a_end - acs_x)).astype(BF16)
    st = _dot_tn(bm, xdw)
    gi = lax.broadcasted_iota(jnp.int32, st.shape, 0) // SSD_STATE
    hi = lax.broadcasted_iota(jnp.int32, st.shape, 1) // (heads_per_group * SSD_HEAD_DIM)
    state_ref[d] = h_in * jnp.exp(a_end) + jnp.where(gi == hi, st, 0.0)


def _split3(x):
    lane = lax.broadcasted_iota(jnp.int32, x.shape, 1)
    r1 = x - x.astype(BF16).astype(F32)
    r2 = r1 - r1.astype(BF16).astype(F32)
    return jnp.where(lane < DT_COLS, x, jnp.where(lane < 2 * DT_COLS, r1, r2)).astype(BF16)


def _ssd_kernel(uf_ref, dtf_ref, ub_ref, dtb_raw_ref, alog_ref, dtbias_ref, skip_ref, tri_ref, eh_ref, ew_ref,
                yf_ref, yb_ref, state_ref):
    s = pl.program_id(1)
    L = SSD_CHUNK

    @pl.when(s == 0)
    def _():
        state_ref[...] = jnp.zeros_like(state_ref)

    neg_a = -jnp.exp(alog_ref[...])
    dtm_f = _softplus(dtf_ref[0] + dtbias_ref[...])
    dtm_b = _softplus(dtb_raw_ref[0] + dtbias_ref[...])
    a_f = dtm_f * neg_a
    a_b = dtm_b * neg_a
    cs = jnp.dot(tri_ref[...], jnp.concatenate([a_f, a_b], axis=1), preferred_element_type=F32,
                 precision=lax.Precision.HIGHEST)
    acs_f = cs[:, 0:LANES]
    cs_b = cs[:, LANES:2 * LANES]
    acs_b = cs_b[L - 1:L, :] - cs_b + a_b

    _ssd_direction(0, uf_ref, dtm_f, acs_f, skip_ref, eh_ref, ew_ref, state_ref, yf_ref)
    _ssd_direction(1, ub_ref, dtm_b, acs_b, skip_ref, eh_ref, ew_ref, state_ref, yb_ref)


def _ssd_constants():
    L = SSD_CHUNK
    tri = np.tril(np.ones((L, L), np.float32))
    e_head = np.zeros((2, LANES, SSD_WIDTH), np.float32)
    e_wide = np.zeros((2, LANES, SSD_HEADS * L), np.float32)
    for d in range(2):
        for rep in range(DT_REPL):
            for h in range(SSD_HEADS):
                r = rep * DT_COLS + d * SSD_HEADS + h
                e_head[d, r, h * SSD_HEAD_DIM:(h + 1) * SSD_HEAD_DIM] = 1.0
                e_wide[d, r, h * L:(h + 1) * L] = 1.0
    return jnp.asarray(tri), jnp.asarray(e_head, BF16), jnp.asarray(e_wide, BF16)


def _ssd(u, dt_raw, alog_row, dtbias_row, skip_x, nctx):
    bsz, tt, cd = u.shape
    L = SSD_CHUNK
    tri, e_head, e_wide = _ssd_constants()
    nchunks = tt // L
    ncc = nctx // L
    n = tt - nctx

    def cf(s):
        return s

    def cb(s):
        return jnp.where(s < ncc, ncc - 1 - s, nchunks - 1 - (s - ncc))

    def specs(cfn):
        return [
            pl.BlockSpec((1, L, cd), lambda b, s: (b, cfn(s), 0)),
            pl.BlockSpec((1, L, LANES), lambda b, s: (b, cfn(s), 0)),
        ]

    const = lambda b, s: (0, 0)
    in_specs = specs(cf) + specs(cb) + [
        pl.BlockSpec((1, LANES), const),
        pl.BlockSpec((1, LANES), const),
        pl.BlockSpec(skip_x.shape, const),
        pl.BlockSpec(tri.shape, const),
        pl.BlockSpec(e_head.shape, lambda b, s: (0, 0, 0)),
        pl.BlockSpec(e_wide.shape, lambda b, s: (0, 0, 0)),
    ]
    out_specs = (
        pl.BlockSpec((1, L, SSD_WIDTH), lambda b, s: (b, jnp.maximum(cf(s) - ncc, 0), 0)),
        pl.BlockSpec((1, L, SSD_WIDTH), lambda b, s: (b, nchunks - 1 - ncc - jnp.maximum(s - ncc, 0), 0)),
    )
    y_shape = jax.ShapeDtypeStruct((bsz, n, SSD_WIDTH), F32)
    return pl.pallas_call(
        _ssd_kernel,
        out_shape=(y_shape, y_shape),
        grid=(bsz, nchunks),
        in_specs=in_specs,
        out_specs=out_specs,
        scratch_shapes=[pltpu.VMEM((2, SSD_GROUPS * SSD_STATE, SSD_WIDTH), F32)],
        compiler_params=_params(("arbitrary", "arbitrary")),
        name="ssd",
    )(u, dt_raw, u, dt_raw, alog_row, dtbias_row, skip_x, tri, e_head, e_wide)


def _mix_out_kernel(nbatch, a_ref, yf_ref, yb_ref, sz_ref, x_ref, mod0_ref, mod1_ref, sn_ref, woa_ref, wos_ref,
                    nw1_ref, wp_ref, x1_ref, u_ref, sg_ref):
    b = pl.program_id(0)
    y = (yf_ref[0] + yb_ref[0]) * sz_ref[0]
    ssd = _rms(y, sn_ref[...]).astype(BF16)
    o = _dot_tn(a_ref[0], woa_ref[...]) + _dot(ssd, wos_ref[...])
    gate0 = mod0_ref[pl.ds(b, 1), 2 * D_MODEL:3 * D_MODEL]
    x1 = x_ref[0] + gate0 * o
    x1_ref[0] = x1
    mod1 = mod1_ref[pl.ds(b, 1), :]
    h1 = _rms(x1, nw1_ref[...]) * (1.0 + mod1[:, D_MODEL:2 * D_MODEL]) + mod1[:, 0:D_MODEL]
    ug = _dot(h1.astype(BF16), wp_ref[...])
    u_ref[0] = ug[:, 0:POOL_WIDTH]
    sg_ref[0] = _silu(ug[:, POOL_WIDTH:2 * POOL_WIDTH])


def _mix_out(a_t, yf, yb, sz, x, mod0, mod1, ssd_norm, wo_a, wo_s, norm_w1, w_pool, tm):
    bsz, n, d = x.shape
    rows = mod0.shape[0]
    const = lambda b, t: (0, 0)
    tile = lambda w: pl.BlockSpec((1, tm, w), lambda b, t: (b, t, 0))
    return pl.pallas_call(
        functools.partial(_mix_out_kernel, bsz),
        out_shape=(jax.ShapeDtypeStruct((bsz, n, d), F32),
                   jax.ShapeDtypeStruct((bsz, n, POOL_WIDTH), F32),
                   jax.ShapeDtypeStruct((bsz, n, POOL_WIDTH), F32)),
        grid=(bsz, n // tm),
        in_specs=[
            pl.BlockSpec((1, MLA_WIDTH, tm), lambda b, t: (b, 0, t)),
            tile(SSD_WIDTH), tile(SSD_WIDTH), tile(SSD_WIDTH), tile(d),
            pl.BlockSpec((rows, 3 * d), const),
            pl.BlockSpec((rows, 3 * d), const),
            pl.BlockSpec((1, SSD_WIDTH), const),
            pl.BlockSpec(wo_a.shape, const),
            pl.BlockSpec(wo_s.shape, const),
            pl.BlockSpec((1, d), const),
            pl.BlockSpec(w_pool.shape, const),
        ],
        out_specs=(tile(d), tile(POOL_WIDTH), tile(POOL_WIDTH)),
        compiler_params=_params(("arbitrary", "arbitrary")),
        name="mix_out",
    )(a_t, yf, yb, sz, x, mod0, mod1, ssd_norm, wo_a, wo_s, norm_w1, w_pool)


def _pool_kernel(n, u_ref, up_ref, un_ref, sg_ref, x1_ref, mod1_ref, lin_ref, ps_ref, wo_ref, fn_ref,
                 o_ref, ext_ref):
    b = pl.program_id(0)
    t = pl.program_id(1)
    nt = pl.num_programs(1)
    tm = u_ref.shape[1]
    H = POOL_HALO
    ext_ref[0:H, :] = jnp.where(t == 0, 0.0, up_ref[0])
    ext_ref[H:H + tm, :] = u_ref[0]
    ext_ref[H + tm:H + tm + H, :] = jnp.where(t == nt - 1, 0.0, un_ref[0])

    tok = t * tm + lax.broadcasted_iota(jnp.int32, (tm, 1), 0)
    ys = []
    for g, w in enumerate(POOL_WINDOWS):
        lanes = slice(g * POOL_GROUP_DIM, (g + 1) * POOL_GROUP_DIM)
        acc = ext_ref[H - w // 2:H - w // 2 + tm, lanes]
        for j in range(1, w):
            acc = acc + ext_ref[H - w // 2 + j:H - w // 2 + j + tm, lanes]
        lo = jnp.maximum(tok - w // 2, 0)
        hi = jnp.minimum(tok + (w - w // 2 - 1), n - 1)
        cnt = (hi - lo + 1).astype(F32)
        m = acc / cnt - u_ref[0, :, lanes]
        ys.append(_dot(m.astype(BF16), lin_ref[g]))
    y = jnp.concatenate(ys, axis=-1) * ps_ref[...] * sg_ref[0]
    o = _dot(y.astype(BF16), wo_ref[...])
    gate1 = mod1_ref[pl.ds(b, 1), 2 * D_MODEL:3 * D_MODEL]
    x2 = x1_ref[0] + gate1 * o
    o_ref[0] = _rms(x2, fn_ref[...])


def _pool_out(u, sg, x1, mod1, pool_lin, pool_scale, wo_pool, final_norm, tm):
    bsz, n, d = x1.shape
    rows = mod1.shape[0]
    r8 = tm // SUBLANES
    nrow8 = n // SUBLANES
    const = lambda b, t: (0, 0)
    tile = lambda w: pl.BlockSpec((1, tm, w), lambda b, t: (b, t, 0))
    return pl.pallas_call(
        functools.partial(_pool_kernel, n),
        out_shape=jax.ShapeDtypeStruct((bsz, n, d), F32),
        grid=(bsz, n // tm),
        in_specs=[
            tile(POOL_WIDTH),
            pl.BlockSpec((1, POOL_HALO, POOL_WIDTH), lambda b, t: (b, jnp.maximum(t * r8 - 1, 0), 0)),
            pl.BlockSpec((1, POOL_HALO, POOL_WIDTH), lambda b, t: (b, jnp.minimum((t + 1) * r8, nrow8 - 1), 0)),
            tile(POOL_WIDTH), tile(d),
            pl.BlockSpec((rows, 3 * d), const),
            pl.BlockSpec(pool_lin.shape, lambda b, t: (0, 0, 0)),
            pl.BlockSpec((1, POOL_WIDTH), const),
            pl.BlockSpec(wo_pool.shape, const),
            pl.BlockSpec((1, d), const),
        ],
        out_specs=tile(d),
        scratch_shapes=[pltpu.VMEM((tm + 2 * POOL_HALO, POOL_WIDTH), F32)],
        compiler_params=_params(("arbitrary", "arbitrary")),
        name="pool_out",
    )(u, u, u, sg, x1, mod1, pool_lin, pool_scale, wo_pool, final_norm)


def _rope_tables(n, nctx):
    rows = n // GRID_W
    row = jnp.repeat(jnp.arange(rows, dtype=F32), GRID_W)
    col = jnp.tile(jnp.arange(GRID_W, dtype=F32), rows)
    axis_dim = MLA_ROPE // 2
    inv_freq = 1.0 / (ROPE_BASE ** (jnp.arange(0, axis_dim, 2, dtype=F32) / axis_dim))
    ang = jnp.concatenate([row[:, None] * inv_freq, col[:, None] * inv_freq], axis=-1)
    cos, sin = jnp.cos(ang), jnp.sin(ang)
    half = MLA_ROPE // 2
    cos_all = jnp.concatenate([jnp.ones((nctx, half), F32), cos], axis=0)
    sin_all = jnp.concatenate([jnp.zeros((nctx, half), F32), sin], axis=0)
    left = jnp.zeros((nctx + n, MLA_NOPE), F32)
    right = jnp.zeros((nctx + n, LANES - MLA_NOPE - MLA_ROPE), F32)
    cos_k = jnp.concatenate([left, cos_all, cos_all, right], axis=1)
    sin_k = jnp.concatenate([left, -sin_all, sin_all, right], axis=1)
    return cos.T, sin.T, cos_k, sin_k


def _pad_row(v):
    v = v.reshape(1, -1).astype(F32)
    return jnp.pad(v, ((0, 0), (0, LANES - v.shape[1])))


def kernel(x, c, ctx, c_ctx, mod_w, mod_b, norm_w, w_in_mix, q_norm, w_uq, kv_norm, w_ukv, conv_w, conv_b,
           a_log, dt_bias, d_skip, ssd_norm, w_out_mix, w_in_pool, pool_lin, pool_scale, w_out_pool, final_norm):
    bsz, n, d = x.shape
    nctx = ctx.shape[1]
    tm = 256
    tm_out = 512
    assert d == D_MODEL and n % tm_out == 0 and nctx % tm == 0 and n % GRID_W == 0
    assert mod_w.shape[0] == 2, "two layers: one mixing layer followed by one pooling layer"

    rows = -(-(bsz + 1) // SUBLANES) * SUBLANES
    cvec = jnp.zeros((rows, d), F32).at[:bsz].set(c).at[bsz].set(c_ctx)
    mods = _modulation(cvec, mod_w, mod_b)
    mod0, mod1 = mods[0], mods[1]

    w = w_in_mix[0]
    offs = np.cumsum((0,) + MIX_SPLITS)
    w_qa, w_kva, w_kpe, w_ga, w_z, w_xbc, w_dt = [w[:, offs[i]:offs[i + 1]] for i in range(len(MIX_SPLITS))]
    half = MLA_ROPE // 2
    kpe_pad = jnp.zeros((d, LANES), F32).at[:, MLA_NOPE:MLA_NOPE + MLA_ROPE].set(w_kpe)
    kpe_swap = jnp.concatenate([w_kpe[:, half:], w_kpe[:, :half]], axis=1)
    kps_pad = jnp.zeros((d, LANES), F32).at[:, MLA_NOPE:MLA_NOPE + MLA_ROPE].set(kpe_swap)
    dt_pad = jnp.pad(jnp.tile(w_dt, (1, DT_REPL)), ((0, 0), (0, LANES - DT_REPL * DT_COLS)))
    wn = jnp.concatenate([w_qa, w_kva, kpe_pad, kps_pad, w_z, w_xbc, dt_pad], axis=1).astype(BF16)
    wg_t = w_ga.T.astype(BF16)
    wuq = w_uq[0].reshape(MLA_Q_RANK, MLA_HEADS, MLA_NOPE + MLA_ROPE)
    wuq = jnp.pad(wuq, ((0, 0), (0, 0), (0, HEAD_PAD - MLA_NOPE - MLA_ROPE)))
    wuq_t = wuq.reshape(MLA_Q_RANK, MLA_HEADS * HEAD_PAD).T.astype(BF16)
    wukv = w_ukv[0].reshape(MLA_KV_RANK, MLA_HEADS, MLA_NOPE + MLA_V)
    wuk_p = jnp.pad(wukv[:, :, :MLA_NOPE], ((0, 0), (0, 0), (0, HEAD_PAD - MLA_NOPE)))
    wuk_p = wuk_p.reshape(MLA_KV_RANK, MLA_HEADS * HEAD_PAD).astype(BF16)
    wuv_t = wukv[:, :, MLA_NOPE:].reshape(MLA_KV_RANK, MLA_HEADS * MLA_V).T.astype(BF16)
    cos_q, sin_q, cos_k, sin_k = _rope_tables(n, nctx)

    q_t, kcat, v_t, g_t, sz, u_ssd, dt_raw = _mix_in(
        x, ctx, mod0, norm_w[0:1], wn, wg_t, q_norm[0:1], kv_norm[0:1], wuq_t, wuk_p, wuv_t,
        cos_q, sin_q, cos_k, sin_k, conv_w[0], conv_b[0:1], tm)

    a_t = _attention(q_t, kcat, v_t, g_t)

    yf, yb = _ssd(u_ssd, dt_raw, _pad_row(jnp.tile(a_log[0].reshape(-1), DT_REPL)),
                  _pad_row(jnp.tile(dt_bias[0].reshape(-1), DT_REPL)),
                  jnp.repeat(d_skip[0].astype(F32), SSD_HEAD_DIM, axis=1), nctx)

    wo = w_out_mix[0].astype(BF16)
    x1, u, sg = _mix_out(a_t, yf, yb, sz, x, mod0, mod1, ssd_norm[0:1], wo[:MLA_WIDTH], wo[MLA_WIDTH:],
                         norm_w[1:2], w_in_pool[0].astype(BF16), tm_out)

    return _pool_out(u, sg, x1, mod1, pool_lin[0].astype(BF16), pool_scale[0:1], w_out_pool[0].astype(BF16),
                     final_norm.reshape(1, d), tm_out)
```

```python
import functools
import math

import jax
import jax.numpy as jnp
import numpy as np
from jax import lax
from jax.experimental import pallas as pl
from jax.experimental.pallas import tpu as pltpu

D_MODEL = 1024
GRID_W = 64

MLA_HEADS = 8
MLA_NOPE = 64
MLA_ROPE = 32
MLA_V = 64
MLA_Q_RANK = 384
MLA_KV_RANK = 256
MLA_WIDTH = MLA_HEADS * MLA_V
ROPE_BASE = 10000.0
HEAD_PAD = 128
V_ROWS = 80

SSD_HEADS = 8
SSD_HEAD_DIM = 64
SSD_WIDTH = SSD_HEADS * SSD_HEAD_DIM
SSD_GROUPS = 2
SSD_STATE = 64
SSD_CONV = 3
SSD_CHUNK = 128
SSD_CONV_DIM = SSD_WIDTH + 2 * SSD_GROUPS * SSD_STATE
MIX_SPLITS = (MLA_Q_RANK, MLA_KV_RANK, MLA_ROPE, MLA_WIDTH, SSD_WIDTH, SSD_CONV_DIM, 2 * SSD_HEADS)
SSD_STEP_CHUNKS = 2
DT_COLS = 2 * SSD_HEADS
DT_REPL = 3

POOL_WINDOWS = (2, 4, 8, 16)
POOL_GROUPS = len(POOL_WINDOWS)
POOL_WIDTH = D_MODEL
POOL_GROUP_DIM = POOL_WIDTH // POOL_GROUPS
POOL_HALO = 8

RMS_EPS = 1e-6
LANES = 128
SUBLANES = 8
VMEM_LIMIT = 56 * 1024 * 1024

C_QA = 0
C_KVA = C_QA + MLA_Q_RANK
C_KPE = C_KVA + MLA_KV_RANK
C_KPS = C_KPE + LANES
C_Z = C_KPS + LANES
C_XBC = C_Z + SSD_WIDTH
C_DT = C_XBC + SSD_CONV_DIM
C_END = C_DT + LANES

BF16 = jnp.bfloat16
F32 = jnp.float32
NEG_BIG = -1e30


def _rms(x, w):
    y = x * lax.rsqrt(jnp.mean(x * x, axis=-1, keepdims=True) + RMS_EPS)
    return y * w


def _silu(x):
    return x * (1.0 / (1.0 + jnp.exp(-x)))


def _softplus(x):
    return jnp.maximum(x, 0.0) + jnp.log(1.0 + jnp.exp(-jnp.abs(x)))


def _dot(a, b):
    return jnp.dot(a, b, preferred_element_type=F32)


def _dot_nt(a, b):
    return lax.dot_general(a, b, (((1,), (1,)), ((), ())), preferred_element_type=F32)


def _dot_tn(a, b):
    return lax.dot_general(a, b, (((0,), (0,)), ((), ())), preferred_element_type=F32)


def _params(sem):
    return pltpu.CompilerParams(dimension_semantics=sem, vmem_limit_bytes=VMEM_LIMIT)


def _mod_kernel(c_ref, w_ref, b_ref, o_ref):
    sc = _silu(c_ref[...]).astype(BF16)
    o_ref[0] = _dot(sc, w_ref[0].astype(BF16)) + b_ref[0]


def _modulation(cvec, mod_w, mod_b):
    depth, d, d3 = mod_w.shape
    rows = cvec.shape[0]
    nblk = d3 // d
    return pl.pallas_call(
        _mod_kernel,
        out_shape=jax.ShapeDtypeStruct((depth, rows, d3), F32),
        grid=(depth, nblk),
        in_specs=[
            pl.BlockSpec((rows, d), lambda i, j: (0, 0)),
            pl.BlockSpec((1, d, d), lambda i, j: (i, 0, j)),
            pl.BlockSpec((1, 1, d), lambda i, j: (i, 0, j)),
        ],
        out_specs=pl.BlockSpec((1, rows, d), lambda i, j: (i, 0, j)),
        compiler_params=_params(("arbitrary", "arbitrary")),
        name="modulation",
    )(cvec, mod_w, mod_b.reshape(depth, 1, d3))


def _mix_in_kernel(nct, nt, nbatch, x_ref, ctx_ref, mod_ref, nw_ref, wn_ref, wg_ref, qn_ref, kvn_ref,
                   wuq_ref, wuk_ref, wuv_ref, cosq_ref, sinq_ref, cosk_ref, sink_ref, cw_ref, cb_ref,
                   q_ref, k_ref, v_ref, g_ref, sz_ref, u_ref, dt_ref, xbuf_ref, last_ref):
    b = pl.program_id(0)
    t = pl.program_id(1)

    @pl.when(t == 0)
    def _():
        last_ref[...] = jnp.zeros_like(last_ref)
        xbuf_ref[...] = jnp.zeros_like(xbuf_ref)

    def finish_conv(xbc_new):
        xc = xbuf_ref[(t + 1) % 2]
        xbuf_ref[t % 2] = xbc_new
        tp = t - 1
        tm = xc.shape[0]
        first_seg = jnp.logical_or(tp == 0, tp == nct)
        last_seg = jnp.logical_or(tp == nct - 1, tp == nt - 1)
        prev_row = jnp.where(first_seg, 0.0, last_ref[0:1, :])
        next_row = jnp.where(last_seg, 0.0, xbc_new[0:1, :])
        ridx = lax.broadcasted_iota(jnp.int32, (tm, 1), 0)
        x_m1 = jnp.where(ridx == 0, prev_row, pltpu.roll(xc, 1, axis=0))
        x_p1 = jnp.where(ridx == tm - 1, next_row, pltpu.roll(xc, tm - 1, axis=0))
        cw = cw_ref[...]
        u_ref[0] = _silu(cb_ref[...] + x_m1 * cw[0:1] + xc * cw[1:2] + x_p1 * cw[2:3])
        last_ref[0:1, :] = xc[tm - 1:tm]

    _mix_in_project(nct, nbatch, b, jnp.minimum(t, nt - 1), x_ref, ctx_ref, mod_ref, nw_ref, wn_ref,
                    wg_ref, qn_ref, kvn_ref, wuq_ref, wuk_ref, wuv_ref, cosq_ref, sinq_ref,
                    cosk_ref, sink_ref, q_ref, k_ref, v_ref, g_ref, sz_ref, dt_ref, finish_conv)


def _mix_in_project(nct, nbatch, b, t, x_ref, ctx_ref, mod_ref, nw_ref, wn_ref, wg_ref, qn_ref, kvn_ref,
                    wuq_ref, wuk_ref, wuv_ref, cosq_ref, sinq_ref, cosk_ref, sink_ref,
                    q_ref, k_ref, v_ref, g_ref, sz_ref, dt_ref, finish_conv):
    is_ctx = t < nct
    xin = jnp.where(is_ctx, ctx_ref[0], x_ref[0])
    row = jnp.where(is_ctx, nbatch, b)
    mod = mod_ref[pl.ds(row, 1), :]
    shift = mod[:, 0:D_MODEL]
    scale = mod[:, D_MODEL:2 * D_MODEL]
    h = _rms(xin, nw_ref[...]) * (1.0 + scale) + shift
    hb = h.astype(BF16)
    proj = _dot(hb, wn_ref[...])

    dt_ref[0] = proj[:, C_DT:C_END]
    finish_conv(proj[:, C_XBC:C_DT])

    ckv = _rms(proj[:, C_KVA:C_KPE], kvn_ref[...]).astype(BF16)
    kpad = _dot(ckv, wuk_ref[...])
    kpe = proj[:, C_KPE:C_KPS] * cosk_ref[...] + proj[:, C_KPS:C_Z] * sink_ref[...]
    vt = _dot_nt(wuv_ref[...], ckv)
    tm = vt.shape[1]
    ones_rows = (lax.broadcasted_iota(jnp.int32, (V_ROWS - MLA_V, tm), 0) == 0).astype(BF16)
    for hd in range(MLA_HEADS):
        k_ref[0, hd] = (kpad[:, hd * HEAD_PAD:(hd + 1) * HEAD_PAD] + kpe).astype(BF16)
        v_ref[0, hd, 0, 0:MLA_V, :] = vt[hd * MLA_V:(hd + 1) * MLA_V].astype(BF16)
        v_ref[0, hd, 0, MLA_V:V_ROWS, :] = ones_rows

    @pl.when(jnp.logical_not(is_ctx))
    def _():
        sz_ref[0] = _silu(proj[:, C_Z:C_XBC])
        g_ref[0] = _silu(_dot_nt(wg_ref[...], hb))
        cq = _rms(proj[:, C_QA:C_KVA], qn_ref[...]).astype(BF16)
        qt = _dot_nt(wuq_ref[...], cq)
        sm_scale = (MLA_NOPE + MLA_ROPE) ** -0.5 * math.log2(math.e)
        cos = cosq_ref[...]
        sin = sinq_ref[...]
        half = MLA_ROPE // 2
        for hd in range(MLA_HEADS):
            base = hd * HEAD_PAD
            t1 = qt[base + MLA_NOPE:base + MLA_NOPE + half]
            t2 = qt[base + MLA_NOPE + half:base + MLA_NOPE + MLA_ROPE]
            q_ref[0, hd, 0, 0:MLA_NOPE, :] = (qt[base:base + MLA_NOPE] * sm_scale).astype(BF16)
            q_ref[0, hd, 0, MLA_NOPE:MLA_NOPE + half, :] = ((t1 * cos - t2 * sin) * sm_scale).astype(BF16)
            q_ref[0, hd, 0, MLA_NOPE + half:MLA_NOPE + MLA_ROPE, :] = ((t2 * cos + t1 * sin) * sm_scale).astype(BF16)
            q_ref[0, hd, 0, MLA_NOPE + MLA_ROPE:HEAD_PAD, :] = jnp.zeros((HEAD_PAD - MLA_NOPE - MLA_ROPE, tm), BF16)


def _mix_in(x, ctx, mod0, norm_w, wn, wg_t, q_norm, kv_norm, wuq_t, wuk_p, wuv_t,
            cos_q, sin_q, cos_k, sin_k, conv_w, conv_b, tm):
    bsz, n, d = x.shape
    nctx = ctx.shape[1]
    nct = nctx // tm
    nlt = n // tm
    nt = nct + nlt
    tt = nctx + n
    rows = mod0.shape[0]

    def lat(t):
        return jnp.clip(t - nct, 0, nlt - 1)

    def cur(t):
        return jnp.minimum(t, nt - 1)

    const = lambda b, t: (0, 0)
    out_shape = (
        jax.ShapeDtypeStruct((bsz, MLA_HEADS, n // tm, HEAD_PAD, tm), BF16),
        jax.ShapeDtypeStruct((bsz, MLA_HEADS, tt, HEAD_PAD), BF16),
        jax.ShapeDtypeStruct((bsz, MLA_HEADS, tt // tm, V_ROWS, tm), BF16),
        jax.ShapeDtypeStruct((bsz, MLA_WIDTH, n), F32),
        jax.ShapeDtypeStruct((bsz, n, SSD_WIDTH), F32),
        jax.ShapeDtypeStruct((bsz, tt, SSD_CONV_DIM), F32),
        jax.ShapeDtypeStruct((bsz, tt, LANES), F32),
    )
    out_specs = (
        pl.BlockSpec((1, MLA_HEADS, 1, HEAD_PAD, tm), lambda b, t: (b, 0, lat(t), 0, 0)),
        pl.BlockSpec((1, MLA_HEADS, tm, HEAD_PAD), lambda b, t: (b, 0, cur(t), 0)),
        pl.BlockSpec((1, MLA_HEADS, 1, V_ROWS, tm), lambda b, t: (b, 0, cur(t), 0, 0)),
        pl.BlockSpec((1, MLA_WIDTH, tm), lambda b, t: (b, 0, lat(t))),
        pl.BlockSpec((1, tm, SSD_WIDTH), lambda b, t: (b, lat(t), 0)),
        pl.BlockSpec((1, tm, SSD_CONV_DIM), lambda b, t: (b, jnp.maximum(t - 1, 0), 0)),
        pl.BlockSpec((1, tm, LANES), lambda b, t: (b, cur(t), 0)),
    )
    in_specs = [
        pl.BlockSpec((1, tm, d), lambda b, t: (b, lat(t), 0)),
        pl.BlockSpec((1, tm, d), lambda b, t: (b, jnp.minimum(t, nct - 1), 0)),
        pl.BlockSpec((rows, 3 * d), const),
        pl.BlockSpec((1, d), const),
        pl.BlockSpec(wn.shape, const),
        pl.BlockSpec(wg_t.shape, const),
        pl.BlockSpec((1, MLA_Q_RANK), const),
        pl.BlockSpec((1, MLA_KV_RANK), const),
        pl.BlockSpec(wuq_t.shape, const),
        pl.BlockSpec(wuk_p.shape, const),
        pl.BlockSpec(wuv_t.shape, const),
        pl.BlockSpec((MLA_ROPE // 2, tm), lambda b, t: (0, lat(t))),
        pl.BlockSpec((MLA_ROPE // 2, tm), lambda b, t: (0, lat(t))),
        pl.BlockSpec((tm, LANES), lambda b, t: (cur(t), 0)),
        pl.BlockSpec((tm, LANES), lambda b, t: (cur(t), 0)),
        pl.BlockSpec(conv_w.shape, const),
        pl.BlockSpec(conv_b.shape, const),
    ]
    return pl.pallas_call(
        functools.partial(_mix_in_kernel, nct, nt, bsz),
        out_shape=out_shape,
        grid=(bsz, nt + 1),
        in_specs=in_specs,
        out_specs=out_specs,
        scratch_shapes=[pltpu.VMEM((2, tm, SSD_CONV_DIM), F32), pltpu.VMEM((SUBLANES, SSD_CONV_DIM), F32)],
        compiler_params=_params(("arbitrary", "arbitrary")),
        name="mix_in",
    )(x, ctx, mod0, norm_w, wn, wg_t, q_norm, kv_norm, wuq_t, wuk_p, wuv_t, cos_q, sin_q, cos_k, sin_k,
      conv_w, conv_b)


def _attn_kernel(nkb, sub, kt, ntiles, q_ref, k_ref, v_ref, g_ref, qn_ref, kn_ref, o_ref, sa_ref, sb_ref, ma_ref):
    i = pl.program_id(2)
    first_step = jnp.logical_and(jnp.logical_and(pl.program_id(0) == 0, pl.program_id(1) == 0), i == 0)
    last_of_head = i == ntiles // 2 - 1
    tq = 2 * q_ref.shape[4]
    kb = sub * kt

    def q_tile(t):
        return jnp.concatenate([q_ref[0, 0, 2 * t], q_ref[0, 0, 2 * t + 1]], axis=1)

    def keys(j, look_ahead):
        rows = slice(j * kb, (j + 1) * kb)
        if look_ahead:
            return jnp.where(last_of_head, kn_ref[0, 0, rows, :], k_ref[0, 0, rows, :])
        return k_ref[0, 0, rows, :]

    def scores(j, q, s_ref, look_ahead=False):
        s = _dot(keys(j, look_ahead), q)
        s_ref[j] = s
        while s.shape[0] > SUBLANES and s.shape[0] % (2 * SUBLANES) == 0:
            half = s.shape[0] // 2
            s = jnp.maximum(s[:half], s[half:])
        return jnp.max(s, axis=0, keepdims=True)

    def weighted_values(j, s_ref, m, acc):
        p = jnp.exp2(s_ref[j] - m).astype(BF16)
        for r in range(sub):
            acc = acc + _dot(v_ref[0, 0, j * sub + r], p[r * kt:(r + 1) * kt])
        return acc

    def fused(q_next, w_ref, r_ref, m_read, look_ahead=False):
        acc = jnp.zeros((V_ROWS, tq), F32)
        m_next = jnp.full((1, tq), NEG_BIG, F32)
        for j in range(nkb):
            m_next = jnp.maximum(m_next, scores(j, q_next, w_ref, look_ahead))
            acc = weighted_values(j, r_ref, m_read, acc)
        return acc, m_next

    def finish(acc, lo):
        out = acc[0:MLA_V] / acc[MLA_V:MLA_V + 1]
        o_ref[0, :, lo:lo + tq] = (out * g_ref[0, :, lo:lo + tq]).astype(BF16)

    @pl.when(first_step)
    def _():
        q0 = q_tile(0)
        m0 = jnp.full((1, tq), NEG_BIG, F32)
        for j in range(nkb):
            m0 = jnp.maximum(m0, scores(j, q0, sa_ref))
        ma_ref[0:1, :] = m0

    acc, m_b = fused(q_tile(2 * i + 1), sb_ref, sa_ref, ma_ref[0:1, :])
    finish(acc, 0)
    q_ahead = jnp.concatenate([qn_ref[0, 0, 0], qn_ref[0, 0, 1]], axis=1)
    q_next = jnp.where(last_of_head, q_ahead, q_tile(jnp.minimum(2 * i + 2, ntiles - 1)))
    acc, m_a = fused(q_next, sa_ref, sb_ref, m_b, look_ahead=True)
    finish(acc, tq)
    ma_ref[0:1, :] = m_a


def _attention(q_t, kcat, v_t, g_t):
    bsz, nh, nqt, _, qt = q_t.shape
    tq = 2 * qt
    n = nqt * qt
    ntiles = n // tq
    tt = kcat.shape[2]
    nvb, kt = v_t.shape[2], v_t.shape[4]
    sub = 3 if nvb % 3 == 0 else 1
    nkb = nvb // sub
    assert ntiles % 2 == 0

    def nxt(b, h):
        f = jnp.minimum(b * nh + h + 1, bsz * nh - 1)
        return f // nh, f % nh

    return pl.pallas_call(
        functools.partial(_attn_kernel, nkb, sub, kt, ntiles),
        out_shape=jax.ShapeDtypeStruct((bsz, nh * MLA_V, n), BF16),
        grid=(bsz, nh, ntiles // 2),
        in_specs=[
            pl.BlockSpec((1, 1, nqt, HEAD_PAD, qt), lambda b, h, i: (b, h, 0, 0, 0)),
            pl.BlockSpec((1, 1, tt, HEAD_PAD), lambda b, h, i: (b, h, 0, 0)),
            pl.BlockSpec((1, 1, nvb, V_ROWS, kt), lambda b, h, i: (b, h, 0, 0, 0)),
            pl.BlockSpec((1, MLA_V, 2 * tq), lambda b, h, i: (b, h, i)),
            pl.BlockSpec((1, 1, 2, HEAD_PAD, qt), lambda b, h, i: (*nxt(b, h), 0, 0, 0)),
            pl.BlockSpec((1, 1, tt, HEAD_PAD), lambda b, h, i: (*nxt(b, h), 0, 0)),
        ],
        out_specs=pl.BlockSpec((1, MLA_V, 2 * tq), lambda b, h, i: (b, h, i)),
        scratch_shapes=[pltpu.VMEM((nkb, sub * kt, tq), F32), pltpu.VMEM((nkb, sub * kt, tq), F32),
                        pltpu.VMEM((SUBLANES, tq), F32)],
        compiler_params=_params(("arbitrary", "arbitrary", "arbitrary")),
        name="attention",
    )(q_t, kcat, v_t, g_t, q_t, kcat)


def _ssd_direction(d, u, dtm, acs, skip_ref, eh_ref, ew_ref, state_ref, y_ref, y_row):
    L = SSD_CHUNK
    xs = u[:, 0:SSD_WIDTH]
    gn = SSD_GROUPS * SSD_STATE
    bm = u[:, SSD_WIDTH:SSD_WIDTH + gn].astype(BF16)
    cm = u[:, SSD_WIDTH + gn:SSD_WIDTH + 2 * gn].astype(BF16)

    e_head = eh_ref[d]
    e_wide = ew_ref[d]
    dt_x = _dot(_split3(dtm), e_head)
    acs3 = _split3(acs)
    acs_x = _dot(acs3, e_head)
    acs_t = acs.T
    end = L - 1 if d == 0 else 0
    a_end = acs_x[end:end + 1, :]

    li = lax.broadcasted_iota(jnp.int32, (L, L), 0)
    si = lax.broadcasted_iota(jnp.int32, (L, L), 1)
    mask = (si <= li) if d == 0 else (si >= li)
    lane = lax.broadcasted_iota(jnp.int32, (L, LANES), 1)
    low = lane < SSD_STATE
    zero_b = jnp.zeros((L, LANES), BF16)

    xd = xs * dt_x
    xdb = xd.astype(BF16)
    heads_per_group = SSD_HEADS // SSD_GROUPS
    ys = []
    for g in range(SSD_GROUPS):
        cg = jnp.where(low, cm, zero_b) if g == 0 else jnp.where(low, zero_b, cm)
        cbm = _dot_nt(cg, bm)
        for pr in range(heads_per_group // 2):
            ms = []
            h0 = g * heads_per_group + 2 * pr
            acs_w = _dot(acs3, e_wide[:, h0 * L:(h0 + 2) * L])
            for r in range(2):
                hd = h0 + r
                col = d * SSD_HEADS + hd
                diff = acs_w[:, r * L:(r + 1) * L] - acs_t[col:col + 1, :]
                lmat = jnp.exp(jnp.where(mask, diff, -jnp.inf))
                ms.append((cbm * lmat).astype(BF16))
            pair = (g * heads_per_group) // 2 + pr
            xp = xdb[:, pair * LANES:(pair + 1) * LANES]
            rhs = jnp.concatenate([jnp.where(low, xp, zero_b), jnp.where(low, zero_b, xp)], axis=0)
            ys.append(_dot(jnp.concatenate(ms, axis=1), rhs))
    h_in = state_ref[d]
    y = jnp.concatenate(ys, axis=1) + jnp.exp(acs_x) * _dot(cm, h_in.astype(BF16)) + skip_ref[d:d + 1, :] * xs
    y_ref[0, y_row:y_row + L, :] = y

    xdw = (xd * jnp.exp(a_end - acs_x)).astype(BF16)
    st = _dot_tn(bm, xdw)
    gi = lax.broadcasted_iota(jnp.int32, st.shape, 0) // SSD_STATE
    hi = lax.broadcasted_iota(jnp.int32, st.shape, 1) // (heads_per_group * SSD_HEAD_DIM)
    state_ref[d] = h_in * jnp.exp(a_end) + jnp.where(gi == hi, st, 0.0)


def _split3(x):
    lane = lax.broadcasted_iota(jnp.int32, x.shape, 1)
    r1 = x - x.astype(BF16).astype(F32)
    r2 = r1 - r1.astype(BF16).astype(F32)
    return jnp.where(lane < DT_COLS, x, jnp.where(lane < 2 * DT_COLS, r1, r2)).astype(BF16)


def _ssd_kernel(uf_ref, dtf_ref, ub_ref, dtb_raw_ref, alog_ref, dtbias_ref, skip_ref, tri_ref, eh_ref, ew_ref,
                yf_ref, yb_ref, state_ref):
    s = pl.program_id(1)
    L = SSD_CHUNK
    nsub = SSD_STEP_CHUNKS

    @pl.when(s == 0)
    def _():
        state_ref[...] = jnp.zeros_like(state_ref)

    neg_a = -jnp.exp(alog_ref[...])
    dtm_f = _softplus(dtf_ref[0] + dtbias_ref[...])
    dtm_b = _softplus(dtb_raw_ref[0] + dtbias_ref[...])
    a_f = dtm_f * neg_a
    a_b = dtm_b * neg_a
    rows = [slice(c * L, (c + 1) * L) for c in range(nsub)]
    cs = jnp.dot(tri_ref[...], jnp.concatenate([a_f[r] for r in rows] + [a_b[r] for r in rows], axis=1),
                 preferred_element_type=F32, precision=lax.Precision.HIGHEST)
    uf = uf_ref[0]
    ub = ub_ref[0]
    for c in range(nsub):
        cf, cbk = c, nsub - 1 - c
        acs_f = cs[:, cf * LANES:(cf + 1) * LANES]
        cs_b = cs[:, (nsub + cbk) * LANES:(nsub + cbk + 1) * LANES]
        acs_b = cs_b[L - 1:L, :] - cs_b + a_b[rows[cbk]]
        _ssd_direction(0, uf[rows[cf]], dtm_f[rows[cf]], acs_f, skip_ref, eh_ref, ew_ref, state_ref, yf_ref, cf * L)
        _ssd_direction(1, ub[rows[cbk]], dtm_b[rows[cbk]], acs_b, skip_ref, eh_ref, ew_ref, state_ref, yb_ref,
                       cbk * L)


def _ssd_constants():
    L = SSD_CHUNK
    tri = np.tril(np.ones((L, L), np.float32))
    e_head = np.zeros((2, LANES, SSD_WIDTH), np.float32)
    e_wide = np.zeros((2, LANES, SSD_HEADS * L), np.float32)
    for d in range(2):
        for rep in range(DT_REPL):
            for h in range(SSD_HEADS):
                r = rep * DT_COLS + d * SSD_HEADS + h
                e_head[d, r, h * SSD_HEAD_DIM:(h + 1) * SSD_HEAD_DIM] = 1.0
                e_wide[d, r, h * L:(h + 1) * L] = 1.0
    return jnp.asarray(tri), jnp.asarray(e_head, BF16), jnp.asarray(e_wide, BF16)


def _ssd(u, dt_raw, alog_row, dtbias_row, skip_x, nctx):
    bsz, tt, cd = u.shape
    tri, e_head, e_wide = _ssd_constants()
    L = SSD_STEP_CHUNKS * SSD_CHUNK
    assert nctx % L == 0 and tt % L == 0
    nchunks = tt // L
    ncc = nctx // L
    n = tt - nctx

    def cf(s):
        return s

    def cb(s):
        return jnp.where(s < ncc, ncc - 1 - s, nchunks - 1 - (s - ncc))

    def specs(cfn):
        return [
            pl.BlockSpec((1, L, cd), lambda b, s: (b, cfn(s), 0)),
            pl.BlockSpec((1, L, LANES), lambda b, s: (b, cfn(s), 0)),
        ]

    const = lambda b, s: (0, 0)
    in_specs = specs(cf) + specs(cb) + [
        pl.BlockSpec((1, LANES), const),
        pl.BlockSpec((1, LANES), const),
        pl.BlockSpec(skip_x.shape, const),
        pl.BlockSpec(tri.shape, const),
        pl.BlockSpec(e_head.shape, lambda b, s: (0, 0, 0)),
        pl.BlockSpec(e_wide.shape, lambda b, s: (0, 0, 0)),
    ]
    out_specs = (
        pl.BlockSpec((1, L, SSD_WIDTH), lambda b, s: (b, jnp.maximum(cf(s) - ncc, 0), 0)),
        pl.BlockSpec((1, L, SSD_WIDTH), lambda b, s: (b, nchunks - 1 - ncc - jnp.maximum(s - ncc, 0), 0)),
    )
    y_shape = jax.ShapeDtypeStruct((bsz, n, SSD_WIDTH), F32)
    return pl.pallas_call(
        _ssd_kernel,
        out_shape=(y_shape, y_shape),
        grid=(bsz, nchunks),
        in_specs=in_specs,
        out_specs=out_specs,
        scratch_shapes=[pltpu.VMEM((2, SSD_GROUPS * SSD_STATE, SSD_WIDTH), F32)],
        compiler_params=_params(("arbitrary", "arbitrary")),
        name="ssd",
    )(u, dt_raw, u, dt_raw, alog_row, dtbias_row, skip_x, tri, e_head, e_wide)


def _mix_out_kernel(nbatch, a_ref, yf_ref, yb_ref, sz_ref, x_ref, mod0_ref, mod1_ref, sn_ref, woa_ref, wos_ref,
                    nw1_ref, wp_ref, x1_ref, u_ref, sg_ref):
    b = pl.program_id(0)
    y = (yf_ref[0] + yb_ref[0]) * sz_ref[0]
    ssd = _rms(y, sn_ref[...]).astype(BF16)
    o = _dot_tn(a_ref[0], woa_ref[...]) + _dot(ssd, wos_ref[...])
    gate0 = mod0_ref[pl.ds(b, 1), 2 * D_MODEL:3 * D_MODEL]
    x1 = x_ref[0] + gate0 * o
    x1_ref[0] = x1
    mod1 = mod1_ref[pl.ds(b, 1), :]
    h1 = _rms(x1, nw1_ref[...]) * (1.0 + mod1[:, D_MODEL:2 * D_MODEL]) + mod1[:, 0:D_MODEL]
    ug = _dot(h1.astype(BF16), wp_ref[...])
    u_ref[0] = ug[:, 0:POOL_WIDTH]
    sg_ref[0] = _silu(ug[:, POOL_WIDTH:2 * POOL_WIDTH])


def _mix_out(a_t, yf, yb, sz, x, mod0, mod1, ssd_norm, wo_a, wo_s, norm_w1, w_pool, tm):
    bsz, n, d = x.shape
    rows = mod0.shape[0]
    const = lambda b, t: (0, 0)
    tile = lambda w: pl.BlockSpec((1, tm, w), lambda b, t: (b, t, 0))
    return pl.pallas_call(
        functools.partial(_mix_out_kernel, bsz),
        out_shape=(jax.ShapeDtypeStruct((bsz, n, d), F32),
                   jax.ShapeDtypeStruct((bsz, n, POOL_WIDTH), F32),
                   jax.ShapeDtypeStruct((bsz, n, POOL_WIDTH), F32)),
        grid=(bsz, n // tm),
        in_specs=[
            pl.BlockSpec((1, MLA_WIDTH, tm), lambda b, t: (b, 0, t)),
            tile(SSD_WIDTH), tile(SSD_WIDTH), tile(SSD_WIDTH), tile(d),
            pl.BlockSpec((rows, 3 * d), const),
            pl.BlockSpec((rows, 3 * d), const),
            pl.BlockSpec((1, SSD_WIDTH), const),
            pl.BlockSpec(wo_a.shape, const),
            pl.BlockSpec(wo_s.shape, const),
            pl.BlockSpec((1, d), const),
            pl.BlockSpec(w_pool.shape, const),
        ],
        out_specs=(tile(d), tile(POOL_WIDTH), tile(POOL_WIDTH)),
        compiler_params=_params(("arbitrary", "arbitrary")),
        name="mix_out",
    )(a_t, yf, yb, sz, x, mod0, mod1, ssd_norm, wo_a, wo_s, norm_w1, w_pool)


def _pool_kernel(n, u_ref, up_ref, un_ref, sg_ref, x1_ref, mod1_ref, lin_ref, ps_ref, wo_ref, fn_ref,
                 o_ref, ext_ref):
    b = pl.program_id(0)
    t = pl.program_id(1)
    nt = pl.num_programs(1)
    tm = u_ref.shape[1]
    H = POOL_HALO
    ext_ref[0:H, :] = jnp.where(t == 0, 0.0, up_ref[0])
    ext_ref[H:H + tm, :] = u_ref[0]
    ext_ref[H + tm:H + tm + H, :] = jnp.where(t == nt - 1, 0.0, un_ref[0])

    tok = t * tm + lax.broadcasted_iota(jnp.int32, (tm, 1), 0)
    rows = tm + 2 * H
    ys = []
    for g, w in enumerate(POOL_WINDOWS):
        lanes = slice(g * POOL_GROUP_DIM, (g + 1) * POOL_GROUP_DIM)
        e = ext_ref[:, lanes]
        win = e + pltpu.roll(e, 1, axis=0)
        width = 2
        while width < w:
            win = pltpu.roll(win, width // 2, axis=0) + pltpu.roll(win, rows - width // 2, axis=0)
            width *= 2
        acc = win[H:H + tm]
        lo = jnp.maximum(tok - w // 2, 0)
        hi = jnp.minimum(tok + (w - w // 2 - 1), n - 1)
        cnt = (hi - lo + 1).astype(F32)
        m = acc / cnt - u_ref[0, :, lanes]
        ys.append(_dot(m.astype(BF16), lin_ref[g]))
    y = jnp.concatenate(ys, axis=-1) * ps_ref[...] * sg_ref[0]
    o = _dot(y.astype(BF16), wo_ref[...])
    gate1 = mod1_ref[pl.ds(b, 1), 2 * D_MODEL:3 * D_MODEL]
    x2 = x1_ref[0] + gate1 * o
    o_ref[0] = _rms(x2, fn_ref[...])


def _pool_out(u, sg, x1, mod1, pool_lin, pool_scale, wo_pool, final_norm, tm):
    bsz, n, d = x1.shape
    rows = mod1.shape[0]
    r8 = tm // SUBLANES
    nrow8 = n // SUBLANES
    const = lambda b, t: (0, 0)
    tile = lambda w: pl.BlockSpec((1, tm, w), lambda b, t: (b, t, 0))
    return pl.pallas_call(
        functools.partial(_pool_kernel, n),
        out_shape=jax.ShapeDtypeStruct((bsz, n, d), F32),
        grid=(bsz, n // tm),
        in_specs=[
            tile(POOL_WIDTH),
            pl.BlockSpec((1, POOL_HALO, POOL_WIDTH), lambda b, t: (b, jnp.maximum(t * r8 - 1, 0), 0)),
            pl.BlockSpec((1, POOL_HALO, POOL_WIDTH), lambda b, t: (b, jnp.minimum((t + 1) * r8, nrow8 - 1), 0)),
            tile(POOL_WIDTH), tile(d),
            pl.BlockSpec((rows, 3 * d), const),
            pl.BlockSpec(pool_lin.shape, lambda b, t: (0, 0, 0)),
            pl.BlockSpec((1, POOL_WIDTH), const),
            pl.BlockSpec(wo_pool.shape, const),
            pl.BlockSpec((1, d), const),
        ],
        out_specs=tile(d),
        scratch_shapes=[pltpu.VMEM((tm + 2 * POOL_HALO, POOL_WIDTH), F32)],
        compiler_params=_params(("arbitrary", "arbitrary")),
        name="pool_out",
    )(u, u, u, sg, x1, mod1, pool_lin, pool_scale, wo_pool, final_norm)


def _rope_tables(n, nctx):
    rows = n // GRID_W
    row = jnp.repeat(jnp.arange(rows, dtype=F32), GRID_W)
    col = jnp.tile(jnp.arange(GRID_W, dtype=F32), rows)
    axis_dim = MLA_ROPE // 2
    inv_freq = 1.0 / (ROPE_BASE ** (jnp.arange(0, axis_dim, 2, dtype=F32) / axis_dim))
    ang = jnp.concatenate([row[:, None] * inv_freq, col[:, None] * inv_freq], axis=-1)
    cos, sin = jnp.cos(ang), jnp.sin(ang)
    half = MLA_ROPE // 2
    cos_all = jnp.concatenate([jnp.ones((nctx, half), F32), cos], axis=0)
    sin_all = jnp.concatenate([jnp.zeros((nctx, half), F32), sin], axis=0)
    left = jnp.zeros((nctx + n, MLA_NOPE), F32)
    right = jnp.zeros((nctx + n, LANES - MLA_NOPE - MLA_ROPE), F32)
    cos_k = jnp.concatenate([left, cos_all, cos_all, right], axis=1)
    sin_k = jnp.concatenate([left, -sin_all, sin_all, right], axis=1)
    return cos.T, sin.T, cos_k, sin_k


def _pad_row(v):
    v = v.reshape(1, -1).astype(F32)
    return jnp.pad(v, ((0, 0), (0, LANES - v.shape[1])))


def kernel(x, c, ctx, c_ctx, mod_w, mod_b, norm_w, w_in_mix, q_norm, w_uq, kv_norm, w_ukv, conv_w, conv_b,
           a_log, dt_bias, d_skip, ssd_norm, w_out_mix, w_in_pool, pool_lin, pool_scale, w_out_pool, final_norm):
    bsz, n, d = x.shape
    nctx = ctx.shape[1]
    tm = 256
    tm_out = 512
    assert d == D_MODEL and n % tm_out == 0 and nctx % tm == 0 and n % GRID_W == 0
    assert mod_w.shape[0] == 2, "two layers: one mixing layer followed by one pooling layer"

    rows = -(-(bsz + 1) // SUBLANES) * SUBLANES
    cvec = jnp.zeros((rows, d), F32).at[:bsz].set(c).at[bsz].set(c_ctx)
    mods = _modulation(cvec, mod_w, mod_b)
    mod0, mod1 = mods[0], mods[1]

    w = w_in_mix[0]
    offs = np.cumsum((0,) + MIX_SPLITS)
    w_qa, w_kva, w_kpe, w_ga, w_z, w_xbc, w_dt = [w[:, offs[i]:offs[i + 1]] for i in range(len(MIX_SPLITS))]
    half = MLA_ROPE // 2
    kpe_pad = jnp.zeros((d, LANES), F32).at[:, MLA_NOPE:MLA_NOPE + MLA_ROPE].set(w_kpe)
    kpe_swap = jnp.concatenate([w_kpe[:, half:], w_kpe[:, :half]], axis=1)
    kps_pad = jnp.zeros((d, LANES), F32).at[:, MLA_NOPE:MLA_NOPE + MLA_ROPE].set(kpe_swap)
    dt_pad = jnp.pad(jnp.tile(w_dt, (1, DT_REPL)), ((0, 0), (0, LANES - DT_REPL * DT_COLS)))
    wn = jnp.concatenate([w_qa, w_kva, kpe_pad, kps_pad, w_z, w_xbc, dt_pad], axis=1).astype(BF16)
    wg_t = w_ga.T.astype(BF16)
    wuq = w_uq[0].reshape(MLA_Q_RANK, MLA_HEADS, MLA_NOPE + MLA_ROPE)
    wuq = jnp.pad(wuq, ((0, 0), (0, 0), (0, HEAD_PAD - MLA_NOPE - MLA_ROPE)))
    wuq_t = wuq.reshape(MLA_Q_RANK, MLA_HEADS * HEAD_PAD).T.astype(BF16)
    wukv = w_ukv[0].reshape(MLA_KV_RANK, MLA_HEADS, MLA_NOPE + MLA_V)
    wuk_p = jnp.pad(wukv[:, :, :MLA_NOPE], ((0, 0), (0, 0), (0, HEAD_PAD - MLA_NOPE)))
    wuk_p = wuk_p.reshape(MLA_KV_RANK, MLA_HEADS * HEAD_PAD).astype(BF16)
    wuv_t = wukv[:, :, MLA_NOPE:].reshape(MLA_KV_RANK, MLA_HEADS * MLA_V).T.astype(BF16)
    cos_q, sin_q, cos_k, sin_k = _rope_tables(n, nctx)

    q_t, kcat, v_t, g_t, sz, u_ssd, dt_raw = _mix_in(
        x, ctx, mod0, norm_w[0:1], wn, wg_t, q_norm[0:1], kv_norm[0:1], wuq_t, wuk_p, wuv_t,
        cos_q, sin_q, cos_k, sin_k, conv_w[0], conv_b[0:1], tm)

    a_t = _attention(q_t, kcat, v_t, g_t)

    yf, yb = _ssd(u_ssd, dt_raw, _pad_row(jnp.tile(a_log[0].reshape(-1), DT_REPL)),
                  _pad_row(jnp.tile(dt_bias[0].reshape(-1), DT_REPL)),
                  jnp.repeat(d_skip[0].astype(F32), SSD_HEAD_DIM, axis=1), nctx)

    wo = w_out_mix[0].astype(BF16)
    x1, u, sg = _mix_out(a_t, yf, yb, sz, x, mod0, mod1, ssd_norm[0:1], wo[:MLA_WIDTH], wo[MLA_WIDTH:],
                         norm_w[1:2], w_in_pool[0].astype(BF16), tm_out)

    return _pool_out(u, sg, x1, mod1, pool_lin[0].astype(BF16), pool_scale[0:1], w_out_pool[0].astype(BF16),
                     final_norm.reshape(1, d), tm_out)
```

```python
import functools
import math

import jax
import jax.numpy as jnp
import numpy as np
from jax import lax
from jax.experimental import pallas as pl
from jax.experimental.pallas import tpu as pltpu

D_MODEL = 1024
GRID_W = 64

MLA_HEADS = 8
MLA_NOPE = 64
MLA_ROPE = 32
MLA_V = 64
MLA_Q_RANK = 384
MLA_KV_RANK = 256
MLA_WIDTH = MLA_HEADS * MLA_V
ROPE_BASE = 10000.0
HEAD_PAD = 128
V_ROWS = 80

SSD_HEADS = 8
SSD_HEAD_DIM = 64
SSD_WIDTH = SSD_HEADS * SSD_HEAD_DIM
SSD_GROUPS = 2
SSD_STATE = 64
SSD_CONV = 3
SSD_CHUNK = 128
SSD_CONV_DIM = SSD_WIDTH + 2 * SSD_GROUPS * SSD_STATE
MIX_SPLITS = (MLA_Q_RANK, MLA_KV_RANK, MLA_ROPE, MLA_WIDTH, SSD_WIDTH, SSD_CONV_DIM, 2 * SSD_HEADS)
SSD_STEP_CHUNKS = 2
DT_COLS = 2 * SSD_HEADS
DT_REPL = 3

POOL_WINDOWS = (2, 4, 8, 16)
POOL_GROUPS = len(POOL_WINDOWS)
POOL_WIDTH = D_MODEL
POOL_GROUP_DIM = POOL_WIDTH // POOL_GROUPS
POOL_HALO = 8

RMS_EPS = 1e-6
LANES = 128
SUBLANES = 8
VMEM_LIMIT = 56 * 1024 * 1024

C_QA = 0
C_KVA = C_QA + MLA_Q_RANK
C_KPE = C_KVA + MLA_KV_RANK
C_KPS = C_KPE + LANES
C_Z = C_KPS + LANES
C_XBC = C_Z + SSD_WIDTH
C_DT = C_XBC + SSD_CONV_DIM
C_END = C_DT + LANES

BF16 = jnp.bfloat16
F32 = jnp.float32
NEG_BIG = -1e30


def _rms(x, w):
    y = x * lax.rsqrt(jnp.mean(x * x, axis=-1, keepdims=True) + RMS_EPS)
    return y * w


def _silu(x):
    return x * (1.0 / (1.0 + jnp.exp(-x)))


def _softplus(x):
    return jnp.maximum(x, 0.0) + jnp.log(1.0 + jnp.exp(-jnp.abs(x)))


def _dot(a, b):
    return jnp.dot(a, b, preferred_element_type=F32)


def _dot_nt(a, b):
    return lax.dot_general(a, b, (((1,), (1,)), ((), ())), preferred_element_type=F32)


def _dot_tn(a, b):
    return lax.dot_general(a, b, (((0,), (0,)), ((), ())), preferred_element_type=F32)


def _params(sem):
    return pltpu.CompilerParams(dimension_semantics=sem, vmem_limit_bytes=VMEM_LIMIT)


def _mod_kernel(c_ref, w_ref, b_ref, o_ref):
    sc = _silu(c_ref[...]).astype(BF16)
    o_ref[0] = _dot(sc, w_ref[0].astype(BF16)) + b_ref[0]


def _modulation(cvec, mod_w, mod_b):
    depth, d, d3 = mod_w.shape
    rows = cvec.shape[0]
    nblk = d3 // d
    return pl.pallas_call(
        _mod_kernel,
        out_shape=jax.ShapeDtypeStruct((depth, rows, d3), F32),
        grid=(depth, nblk),
        in_specs=[
            pl.BlockSpec((rows, d), lambda i, j: (0, 0)),
            pl.BlockSpec((1, d, d), lambda i, j: (i, 0, j)),
            pl.BlockSpec((1, 1, d), lambda i, j: (i, 0, j)),
        ],
        out_specs=pl.BlockSpec((1, rows, d), lambda i, j: (i, 0, j)),
        compiler_params=_params(("arbitrary", "arbitrary")),
        name="modulation",
    )(cvec, mod_w, mod_b.reshape(depth, 1, d3))


def _mix_in_kernel(nct, nt, nbatch, x_ref, ctx_ref, mod_ref, nw_ref, wn_ref, wg_ref, qn_ref, kvn_ref,
                   wuq_ref, wuk_ref, wuv_ref, cosq_ref, sinq_ref, cosk_ref, sink_ref, cw_ref, cb_ref,
                   q_ref, k_ref, v_ref, g_ref, sz_ref, u_ref, dt_ref, xbuf_ref, last_ref):
    b = pl.program_id(0)
    t = pl.program_id(1)

    @pl.when(t == 0)
    def _():
        last_ref[...] = jnp.zeros_like(last_ref)
        xbuf_ref[...] = jnp.zeros_like(xbuf_ref)

    def finish_conv(xbc_new):
        xc = xbuf_ref[(t + 1) % 2]
        xbuf_ref[t % 2] = xbc_new
        tp = t - 1
        tm = xc.shape[0]
        first_seg = jnp.logical_or(tp == 0, tp == nct)
        last_seg = jnp.logical_or(tp == nct - 1, tp == nt - 1)
        prev_row = jnp.where(first_seg, 0.0, last_ref[0:1, :])
        next_row = jnp.where(last_seg, 0.0, xbc_new[0:1, :])
        ridx = lax.broadcasted_iota(jnp.int32, (tm, 1), 0)
        x_m1 = jnp.where(ridx == 0, prev_row, pltpu.roll(xc, 1, axis=0))
        x_p1 = jnp.where(ridx == tm - 1, next_row, pltpu.roll(xc, tm - 1, axis=0))
        cw = cw_ref[...]
        u_ref[0] = _silu(cb_ref[...] + x_m1 * cw[0:1] + xc * cw[1:2] + x_p1 * cw[2:3])
        last_ref[0:1, :] = xc[tm - 1:tm]

    _mix_in_project(nct, nbatch, b, jnp.minimum(t, nt - 1), x_ref, ctx_ref, mod_ref, nw_ref, wn_ref,
                    wg_ref, qn_ref, kvn_ref, wuq_ref, wuk_ref, wuv_ref, cosq_ref, sinq_ref,
                    cosk_ref, sink_ref, q_ref, k_ref, v_ref, g_ref, sz_ref, dt_ref, finish_conv)


def _mix_in_project(nct, nbatch, b, t, x_ref, ctx_ref, mod_ref, nw_ref, wn_ref, wg_ref, qn_ref, kvn_ref,
                    wuq_ref, wuk_ref, wuv_ref, cosq_ref, sinq_ref, cosk_ref, sink_ref,
                    q_ref, k_ref, v_ref, g_ref, sz_ref, dt_ref, finish_conv):
    is_ctx = t < nct
    xin = jnp.where(is_ctx, ctx_ref[0], x_ref[0])
    row = jnp.where(is_ctx, nbatch, b)
    mod = mod_ref[pl.ds(row, 1), :]
    shift = mod[:, 0:D_MODEL]
    scale = mod[:, D_MODEL:2 * D_MODEL]
    h = _rms(xin, nw_ref[...]) * (1.0 + scale) + shift
    hb = h.astype(BF16)
    proj = _dot(hb, wn_ref[...])

    dt_ref[0] = proj[:, C_DT:C_END]
    finish_conv(proj[:, C_XBC:C_DT])

    ckv = _rms(proj[:, C_KVA:C_KPE], kvn_ref[...]).astype(BF16)
    kpad = _dot(ckv, wuk_ref[...])
    kpe = proj[:, C_KPE:C_KPS] * cosk_ref[...] + proj[:, C_KPS:C_Z] * sink_ref[...]
    vt = _dot_nt(wuv_ref[...], ckv)
    tm = vt.shape[1]
    ones_rows = (lax.broadcasted_iota(jnp.int32, (V_ROWS - MLA_V, tm), 0) == 0).astype(BF16)
    for hd in range(MLA_HEADS):
        k_ref[0, hd] = (kpad[:, hd * HEAD_PAD:(hd + 1) * HEAD_PAD] + kpe).astype(BF16)
        v_ref[0, hd, 0, 0:MLA_V, :] = vt[hd * MLA_V:(hd + 1) * MLA_V].astype(BF16)
        v_ref[0, hd, 0, MLA_V:V_ROWS, :] = ones_rows

    @pl.when(jnp.logical_not(is_ctx))
    def _():
        sz_ref[0] = _silu(proj[:, C_Z:C_XBC])
        g_ref[0] = _silu(_dot_nt(wg_ref[...], hb))
        cq = _rms(proj[:, C_QA:C_KVA], qn_ref[...]).astype(BF16)
        qt = _dot_nt(wuq_ref[...], cq)
        sm_scale = (MLA_NOPE + MLA_ROPE) ** -0.5 * math.log2(math.e)
        cos = cosq_ref[...]
        sin = sinq_ref[...]
        half = MLA_ROPE // 2
        for hd in range(MLA_HEADS):
            base = hd * HEAD_PAD
            t1 = qt[base + MLA_NOPE:base + MLA_NOPE + half]
            t2 = qt[base + MLA_NOPE + half:base + MLA_NOPE + MLA_ROPE]
            q_ref[0, hd, 0, 0:MLA_NOPE, :] = (qt[base:base + MLA_NOPE] * sm_scale).astype(BF16)
            q_ref[0, hd, 0, MLA_NOPE:MLA_NOPE + half, :] = ((t1 * cos - t2 * sin) * sm_scale).astype(BF16)
            q_ref[0, hd, 0, MLA_NOPE + half:MLA_NOPE + MLA_ROPE, :] = ((t2 * cos + t1 * sin) * sm_scale).astype(BF16)
            q_ref[0, hd, 0, MLA_NOPE + MLA_ROPE:HEAD_PAD, :] = jnp.zeros((HEAD_PAD - MLA_NOPE - MLA_ROPE, tm), BF16)


def _mix_in(x, ctx, mod0, norm_w, wn, wg_t, q_norm, kv_norm, wuq_t, wuk_p, wuv_t,
            cos_q, sin_q, cos_k, sin_k, conv_w, conv_b, tm):
    bsz, n, d = x.shape
    nctx = ctx.shape[1]
    nct = nctx // tm
    nlt = n // tm
    nt = nct + nlt
    tt = nctx + n
    rows = mod0.shape[0]

    def lat(t):
        return jnp.clip(t - nct, 0, nlt - 1)

    def cur(t):
        return jnp.minimum(t, nt - 1)

    const = lambda b, t: (0, 0)
    out_shape = (
        jax.ShapeDtypeStruct((bsz, MLA_HEADS, n // tm, HEAD_PAD, tm), BF16),
        jax.ShapeDtypeStruct((bsz, MLA_HEADS, tt, HEAD_PAD), BF16),
        jax.ShapeDtypeStruct((bsz, MLA_HEADS, tt // tm, V_ROWS, tm), BF16),
        jax.ShapeDtypeStruct((bsz, MLA_WIDTH, n), F32),
        jax.ShapeDtypeStruct((bsz, n, SSD_WIDTH), F32),
        jax.ShapeDtypeStruct((bsz, tt, SSD_CONV_DIM), F32),
        jax.ShapeDtypeStruct((bsz, tt, LANES), F32),
    )
    out_specs = (
        pl.BlockSpec((1, MLA_HEADS, 1, HEAD_PAD, tm), lambda b, t: (b, 0, lat(t), 0, 0)),
        pl.BlockSpec((1, MLA_HEADS, tm, HEAD_PAD), lambda b, t: (b, 0, cur(t), 0)),
        pl.BlockSpec((1, MLA_HEADS, 1, V_ROWS, tm), lambda b, t: (b, 0, cur(t), 0, 0)),
        pl.BlockSpec((1, MLA_WIDTH, tm), lambda b, t: (b, 0, lat(t))),
        pl.BlockSpec((1, tm, SSD_WIDTH), lambda b, t: (b, lat(t), 0)),
        pl.BlockSpec((1, tm, SSD_CONV_DIM), lambda b, t: (b, jnp.maximum(t - 1, 0), 0)),
        pl.BlockSpec((1, tm, LANES), lambda b, t: (b, cur(t), 0)),
    )
    in_specs = [
        pl.BlockSpec((1, tm, d), lambda b, t: (b, lat(t), 0)),
        pl.BlockSpec((1, tm, d), lambda b, t: (b, jnp.minimum(t, nct - 1), 0)),
        pl.BlockSpec((rows, 3 * d), const),
        pl.BlockSpec((1, d), const),
        pl.BlockSpec(wn.shape, const),
        pl.BlockSpec(wg_t.shape, const),
        pl.BlockSpec((1, MLA_Q_RANK), const),
        pl.BlockSpec((1, MLA_KV_RANK), const),
        pl.BlockSpec(wuq_t.shape, const),
        pl.BlockSpec(wuk_p.shape, const),
        pl.BlockSpec(wuv_t.shape, const),
        pl.BlockSpec((MLA_ROPE // 2, tm), lambda b, t: (0, lat(t))),
        pl.BlockSpec((MLA_ROPE // 2, tm), lambda b, t: (0, lat(t))),
        pl.BlockSpec((tm, LANES), lambda b, t: (cur(t), 0)),
        pl.BlockSpec((tm, LANES), lambda b, t: (cur(t), 0)),
        pl.BlockSpec(conv_w.shape, const),
        pl.BlockSpec(conv_b.shape, const),
    ]
    return pl.pallas_call(
        functools.partial(_mix_in_kernel, nct, nt, bsz),
        out_shape=out_shape,
        grid=(bsz, nt + 1),
        in_specs=in_specs,
        out_specs=out_specs,
        scratch_shapes=[pltpu.VMEM((2, tm, SSD_CONV_DIM), F32), pltpu.VMEM((SUBLANES, SSD_CONV_DIM), F32)],
        compiler_params=_params(("arbitrary", "arbitrary")),
        name="mix_in",
    )(x, ctx, mod0, norm_w, wn, wg_t, q_norm, kv_norm, wuq_t, wuk_p, wuv_t, cos_q, sin_q, cos_k, sin_k,
      conv_w, conv_b)


def _attn_kernel(nkb, sub, kt, ntiles, q_ref, k_ref, v_ref, g_ref, qn_ref, kn_ref, o_ref, sa_ref, sb_ref, ma_ref):
    i = pl.program_id(2)
    first_step = jnp.logical_and(jnp.logical_and(pl.program_id(0) == 0, pl.program_id(1) == 0), i == 0)
    last_of_head = i == ntiles // 2 - 1
    tq = 2 * q_ref.shape[4]
    kb = sub * kt

    def q_tile(t):
        return jnp.concatenate([q_ref[0, 0, 2 * t], q_ref[0, 0, 2 * t + 1]], axis=1)

    def keys(j, look_ahead):
        rows = slice(j * kb, (j + 1) * kb)
        if look_ahead:
            return jnp.where(last_of_head, kn_ref[0, 0, rows, :], k_ref[0, 0, rows, :])
        return k_ref[0, 0, rows, :]

    def scores(j, q, s_ref, look_ahead=False):
        s = _dot(keys(j, look_ahead), q)
        s_ref[j] = s
        while s.shape[0] > SUBLANES and s.shape[0] % (2 * SUBLANES) == 0:
            half = s.shape[0] // 2
            s = jnp.maximum(s[:half], s[half:])
        return jnp.max(s, axis=0, keepdims=True)

    def weighted_values(j, s_ref, m, acc):
        p = jnp.exp2(s_ref[j] - m).astype(BF16)
        for r in range(sub):
            acc = acc + _dot(v_ref[0, 0, j * sub + r], p[r * kt:(r + 1) * kt])
        return acc

    def fused(q_next, w_ref, r_ref, m_read, look_ahead=False):
        acc = jnp.zeros((V_ROWS, tq), F32)
        m_next = jnp.full((1, tq), NEG_BIG, F32)
        for j in range(nkb):
            m_next = jnp.maximum(m_next, scores(j, q_next, w_ref, look_ahead))
            acc = weighted_values(j, r_ref, m_read, acc)
        return acc, m_next

    def finish(acc, lo):
        out = acc[0:MLA_V] / acc[MLA_V:MLA_V + 1]
        o_ref[0, :, lo:lo + tq] = (out * g_ref[0, :, lo:lo + tq]).astype(BF16)

    @pl.when(first_step)
    def _():
        q0 = q_tile(0)
        m0 = jnp.full((1, tq), NEG_BIG, F32)
        for j in range(nkb):
            m0 = jnp.maximum(m0, scores(j, q0, sa_ref))
        ma_ref[0:1, :] = m0

    acc, m_b = fused(q_tile(2 * i + 1), sb_ref, sa_ref, ma_ref[0:1, :])
    finish(acc, 0)
    q_ahead = jnp.concatenate([qn_ref[0, 0, 0], qn_ref[0, 0, 1]], axis=1)
    q_next = jnp.where(last_of_head, q_ahead, q_tile(jnp.minimum(2 * i + 2, ntiles - 1)))
    acc, m_a = fused(q_next, sa_ref, sb_ref, m_b, look_ahead=True)
    finish(acc, tq)
    ma_ref[0:1, :] = m_a


def _attention(q_t, kcat, v_t, g_t):
    bsz, nh, nqt, _, qt = q_t.shape
    tq = 2 * qt
    n = nqt * qt
    ntiles = n // tq
    tt = kcat.shape[2]
    nvb, kt = v_t.shape[2], v_t.shape[4]
    sub = 1
    nkb = nvb // sub
    assert ntiles % 2 == 0

    def nxt(b, h):
        f = jnp.minimum(b * nh + h + 1, bsz * nh - 1)
        return f // nh, f % nh

    return pl.pallas_call(
        functools.partial(_attn_kernel, nkb, sub, kt, ntiles),
        out_shape=jax.ShapeDtypeStruct((bsz, nh * MLA_V, n), BF16),
        grid=(bsz, nh, ntiles // 2),
        in_specs=[
            pl.BlockSpec((1, 1, nqt, HEAD_PAD, qt), lambda b, h, i: (b, h, 0, 0, 0)),
            pl.BlockSpec((1, 1, tt, HEAD_PAD), lambda b, h, i: (b, h, 0, 0)),
            pl.BlockSpec((1, 1, nvb, V_ROWS, kt), lambda b, h, i: (b, h, 0, 0, 0)),
            pl.BlockSpec((1, MLA_V, 2 * tq), lambda b, h, i: (b, h, i)),
            pl.BlockSpec((1, 1, 2, HEAD_PAD, qt), lambda b, h, i: (*nxt(b, h), 0, 0, 0)),
            pl.BlockSpec((1, 1, tt, HEAD_PAD), lambda b, h, i: (*nxt(b, h), 0, 0)),
        ],
        out_specs=pl.BlockSpec((1, MLA_V, 2 * tq), lambda b, h, i: (b, h, i)),
        scratch_shapes=[pltpu.VMEM((nkb, sub * kt, tq), F32), pltpu.VMEM((nkb, sub * kt, tq), F32),
                        pltpu.VMEM((SUBLANES, tq), F32)],
        compiler_params=_params(("arbitrary", "arbitrary", "arbitrary")),
        name="attention",
    )(q_t, kcat, v_t, g_t, q_t, kcat)


def _ssd_direction(d, u, dtm, acs, skip_ref, eh_ref, ew_ref, state_ref, y_ref, y_row):
    L = SSD_CHUNK
    xs = u[:, 0:SSD_WIDTH]
    gn = SSD_GROUPS * SSD_STATE
    bm = u[:, SSD_WIDTH:SSD_WIDTH + gn].astype(BF16)
    cm = u[:, SSD_WIDTH + gn:SSD_WIDTH + 2 * gn].astype(BF16)

    e_head = eh_ref[d]
    e_wide = ew_ref[d]
    dt_x = _dot(_split3(dtm), e_head)
    acs3 = _split3(acs)
    acs_x = _dot(acs3, e_head)
    acs_t = acs.T
    end = L - 1 if d == 0 else 0
    a_end = acs_x[end:end + 1, :]

    li = lax.broadcasted_iota(jnp.int32, (L, L), 0)
    si = lax.broadcasted_iota(jnp.int32, (L, L), 1)
    mask = (si <= li) if d == 0 else (si >= li)
    lane = lax.broadcasted_iota(jnp.int32, (L, LANES), 1)
    low = lane < SSD_STATE
    zero_b = jnp.zeros((L, LANES), BF16)

    xd = xs * dt_x
    xdb = xd.astype(BF16)
    heads_per_group = SSD_HEADS // SSD_GROUPS
    ys = []
    for g in range(SSD_GROUPS):
        cg = jnp.where(low, cm, zero_b) if g == 0 else jnp.where(low, zero_b, cm)
        cbm = _dot_nt(cg, bm)
        for pr in range(heads_per_group // 2):
            ms = []
            h0 = g * heads_per_group + 2 * pr
            acs_w = _dot(acs3, e_wide[:, h0 * L:(h0 + 2) * L])
            for r in range(2):
                hd = h0 + r
                col = d * SSD_HEADS + hd
                diff = acs_w[:, r * L:(r + 1) * L] - acs_t[col:col + 1, :]
                lmat = jnp.exp(jnp.where(mask, diff, -jnp.inf))
                ms.append((cbm * lmat).astype(BF16))
            pair = (g * heads_per_group) // 2 + pr
            xp = xdb[:, pair * LANES:(pair + 1) * LANES]
            rhs = jnp.concatenate([jnp.where(low, xp, zero_b), jnp.where(low, zero_b, xp)], axis=0)
            ys.append(_dot(jnp.concatenate(ms, axis=1), rhs))
    h_in = state_ref[d]
    y = jnp.concatenate(ys, axis=1) + jnp.exp(acs_x) * _dot(cm, h_in.astype(BF16)) + skip_ref[d:d + 1, :] * xs
    y_ref[0, y_row:y_row + L, :] = y

    xdw = (xd * jnp.exp(a_end - acs_x)).astype(BF16)
    st = _dot_tn(bm, xdw)
    gi = lax.broadcasted_iota(jnp.int32, st.shape, 0) // SSD_STATE
    hi = lax.broadcasted_iota(jnp.int32, st.shape, 1) // (heads_per_group * SSD_HEAD_DIM)
    state_ref[d] = h_in * jnp.exp(a_end) + jnp.where(gi == hi, st, 0.0)


def _split3(x):
    lane = lax.broadcasted_iota(jnp.int32, x.shape, 1)
    r1 = x - x.astype(BF16).astype(F32)
    r2 = r1 - r1.astype(BF16).astype(F32)
    return jnp.where(lane < DT_COLS, x, jnp.where(lane < 2 * DT_COLS, r1, r2)).astype(BF16)


def _ssd_kernel(uf_ref, dtf_ref, ub_ref, dtb_raw_ref, alog_ref, dtbias_ref, skip_ref, tri_ref, eh_ref, ew_ref,
                yf_ref, yb_ref, state_ref):
    s = pl.program_id(1)
    L = SSD_CHUNK
    nsub = SSD_STEP_CHUNKS

    @pl.when(s == 0)
    def _():
        state_ref[...] = jnp.zeros_like(state_ref)

    neg_a = -jnp.exp(alog_ref[...])
    dtm_f = _softplus(dtf_ref[0] + dtbias_ref[...])
    dtm_b = _softplus(dtb_raw_ref[0] + dtbias_ref[...])
    a_f = dtm_f * neg_a
    a_b = dtm_b * neg_a
    rows = [slice(c * L, (c + 1) * L) for c in range(nsub)]
    cs = jnp.dot(tri_ref[...], jnp.concatenate([a_f[r] for r in rows] + [a_b[r] for r in rows], axis=1),
                 preferred_element_type=F32, precision=lax.Precision.HIGHEST)
    uf = uf_ref[0]
    ub = ub_ref[0]
    for c in range(nsub):
        cf, cbk = c, nsub - 1 - c
        acs_f = cs[:, cf * LANES:(cf + 1) * LANES]
        cs_b = cs[:, (nsub + cbk) * LANES:(nsub + cbk + 1) * LANES]
        acs_b = cs_b[L - 1:L, :] - cs_b + a_b[rows[cbk]]
        _ssd_direction(0, uf[rows[cf]], dtm_f[rows[cf]], acs_f, skip_ref, eh_ref, ew_ref, state_ref, yf_ref, cf * L)
        _ssd_direction(1, ub[rows[cbk]], dtm_b[rows[cbk]], acs_b, skip_ref, eh_ref, ew_ref, state_ref, yb_ref,
                       cbk * L)


def _ssd_constants():
    L = SSD_CHUNK
    tri = np.tril(np.ones((L, L), np.float32))
    e_head = np.zeros((2, LANES, SSD_WIDTH), np.float32)
    e_wide = np.zeros((2, LANES, SSD_HEADS * L), np.float32)
    for d in range(2):
        for rep in range(DT_REPL):
            for h in range(SSD_HEADS):
                r = rep * DT_COLS + d * SSD_HEADS + h
                e_head[d, r, h * SSD_HEAD_DIM:(h + 1) * SSD_HEAD_DIM] = 1.0
                e_wide[d, r, h * L:(h + 1) * L] = 1.0
    return jnp.asarray(tri), jnp.asarray(e_head, BF16), jnp.asarray(e_wide, BF16)


def _ssd(u, dt_raw, alog_row, dtbias_row, skip_x, nctx):
    bsz, tt, cd = u.shape
    tri, e_head, e_wide = _ssd_constants()
    L = SSD_STEP_CHUNKS * SSD_CHUNK
    assert nctx % L == 0 and tt % L == 0
    nchunks = tt // L
    ncc = nctx // L
    n = tt - nctx

    def cf(s):
        return s

    def cb(s):
        return jnp.where(s < ncc, ncc - 1 - s, nchunks - 1 - (s - ncc))

    def specs(cfn):
        return [
            pl.BlockSpec((1, L, cd), lambda b, s: (b, cfn(s), 0)),
            pl.BlockSpec((1, L, LANES), lambda b, s: (b, cfn(s), 0)),
        ]

    const = lambda b, s: (0, 0)
    in_specs = specs(cf) + specs(cb) + [
        pl.BlockSpec((1, LANES), const),
        pl.BlockSpec((1, LANES), const),
        pl.BlockSpec(skip_x.shape, const),
        pl.BlockSpec(tri.shape, const),
        pl.BlockSpec(e_head.shape, lambda b, s: (0, 0, 0)),
        pl.BlockSpec(e_wide.shape, lambda b, s: (0, 0, 0)),
    ]
    out_specs = (
        pl.BlockSpec((1, L, SSD_WIDTH), lambda b, s: (b, jnp.maximum(cf(s) - ncc, 0), 0)),
        pl.BlockSpec((1, L, SSD_WIDTH), lambda b, s: (b, nchunks - 1 - ncc - jnp.maximum(s - ncc, 0), 0)),
    )
    y_shape = jax.ShapeDtypeStruct((bsz, n, SSD_WIDTH), F32)
    return pl.pallas_call(
        _ssd_kernel,
        out_shape=(y_shape, y_shape),
        grid=(bsz, nchunks),
        in_specs=in_specs,
        out_specs=out_specs,
        scratch_shapes=[pltpu.VMEM((2, SSD_GROUPS * SSD_STATE, SSD_WIDTH), F32)],
        compiler_params=_params(("arbitrary", "arbitrary")),
        name="ssd",
    )(u, dt_raw, u, dt_raw, alog_row, dtbias_row, skip_x, tri, e_head, e_wide)


def _mix_out_kernel(nbatch, a_ref, yf_ref, yb_ref, sz_ref, x_ref, mod0_ref, mod1_ref, sn_ref, woa_ref, wos_ref,
                    nw1_ref, wp_ref, x1_ref, u_ref, sg_ref):
    b = pl.program_id(0)
    y = (yf_ref[0] + yb_ref[0]) * sz_ref[0]
    ssd = _rms(y, sn_ref[...]).astype(BF16)
    o = _dot_tn(a_ref[0], woa_ref[...]) + _dot(ssd, wos_ref[...])
    gate0 = mod0_ref[pl.ds(b, 1), 2 * D_MODEL:3 * D_MODEL]
    x1 = x_ref[0] + gate0 * o
    x1_ref[0] = x1
    mod1 = mod1_ref[pl.ds(b, 1), :]
    h1 = _rms(x1, nw1_ref[...]) * (1.0 + mod1[:, D_MODEL:2 * D_MODEL]) + mod1[:, 0:D_MODEL]
    ug = _dot(h1.astype(BF16), wp_ref[...])
    u_ref[0] = ug[:, 0:POOL_WIDTH]
    sg_ref[0] = _silu(ug[:, POOL_WIDTH:2 * POOL_WIDTH])


def _mix_out(a_t, yf, yb, sz, x, mod0, mod1, ssd_norm, wo_a, wo_s, norm_w1, w_pool, tm):
    bsz, n, d = x.shape
    rows = mod0.shape[0]
    const = lambda b, t: (0, 0)
    tile = lambda w: pl.BlockSpec((1, tm, w), lambda b, t: (b, t, 0))
    return pl.pallas_call(
        functools.partial(_mix_out_kernel, bsz),
        out_shape=(jax.ShapeDtypeStruct((bsz, n, d), F32),
                   jax.ShapeDtypeStruct((bsz, n, POOL_WIDTH), F32),
                   jax.ShapeDtypeStruct((bsz, n, POOL_WIDTH), F32)),
        grid=(bsz, n // tm),
        in_specs=[
            pl.BlockSpec((1, MLA_WIDTH, tm), lambda b, t: (b, 0, t)),
            tile(SSD_WIDTH), tile(SSD_WIDTH), tile(SSD_WIDTH), tile(d),
            pl.BlockSpec((rows, 3 * d), const),
            pl.BlockSpec((rows, 3 * d), const),
            pl.BlockSpec((1, SSD_WIDTH), const),
            pl.BlockSpec(wo_a.shape, const),
            pl.BlockSpec(wo_s.shape, const),
            pl.BlockSpec((1, d), const),
            pl.BlockSpec(w_pool.shape, const),
        ],
        out_specs=(tile(d), tile(POOL_WIDTH), tile(POOL_WIDTH)),
        compiler_params=_params(("arbitrary", "arbitrary")),
        name="mix_out",
    )(a_t, yf, yb, sz, x, mod0, mod1, ssd_norm, wo_a, wo_s, norm_w1, w_pool)


def _pool_kernel(n, u_ref, up_ref, un_ref, sg_ref, x1_ref, mod1_ref, lin_ref, ps_ref, wo_ref, fn_ref,
                 o_ref, ext_ref):
    b = pl.program_id(0)
    t = pl.program_id(1)
    nt = pl.num_programs(1)
    tm = u_ref.shape[1]
    H = POOL_HALO
    ext_ref[0:H, :] = jnp.where(t == 0, 0.0, up_ref[0])
    ext_ref[H:H + tm, :] = u_ref[0]
    ext_ref[H + tm:H + tm + H, :] = jnp.where(t == nt - 1, 0.0, un_ref[0])

    tok = t * tm + lax.broadcasted_iota(jnp.int32, (tm, 1), 0)
    rows = tm + 2 * H
    ys = []
    for g, w in enumerate(POOL_WINDOWS):
        lanes = slice(g * POOL_GROUP_DIM, (g + 1) * POOL_GROUP_DIM)
        e = ext_ref[:, lanes]
        win = e + pltpu.roll(e, 1, axis=0)
        width = 2
        while width < w:
            win = pltpu.roll(win, width // 2, axis=0) + pltpu.roll(win, rows - width // 2, axis=0)
            width *= 2
        acc = win[H:H + tm]
        lo = jnp.maximum(tok - w // 2, 0)
        hi = jnp.minimum(tok + (w - w // 2 - 1), n - 1)
        cnt = (hi - lo + 1).astype(F32)
        m = acc / cnt - u_ref[0, :, lanes]
        ys.append(_dot(m.astype(BF16), lin_ref[g]))
    y = jnp.concatenate(ys, axis=-1) * ps_ref[...] * sg_ref[0]
    o = _dot(y.astype(BF16), wo_ref[...])
    gate1 = mod1_ref[pl.ds(b, 1), 2 * D_MODEL:3 * D_MODEL]
    x2 = x1_ref[0] + gate1 * o
    o_ref[0] = _rms(x2, fn_ref[...])


def _pool_out(u, sg, x1, mod1, pool_lin, pool_scale, wo_pool, final_norm, tm):
    bsz, n, d = x1.shape
    rows = mod1.shape[0]
    r8 = tm // SUBLANES
    nrow8 = n // SUBLANES
    const = lambda b, t: (0, 0)
    tile = lambda w: pl.BlockSpec((1, tm, w), lambda b, t: (b, t, 0))
    return pl.pallas_call(
        functools.partial(_pool_kernel, n),
        out_shape=jax.ShapeDtypeStruct((bsz, n, d), F32),
        grid=(bsz, n // tm),
        in_specs=[
            tile(POOL_WIDTH),
            pl.BlockSpec((1, POOL_HALO, POOL_WIDTH), lambda b, t: (b, jnp.maximum(t * r8 - 1, 0), 0)),
            pl.BlockSpec((1, POOL_HALO, POOL_WIDTH), lambda b, t: (b, jnp.minimum((t + 1) * r8, nrow8 - 1), 0)),
            tile(POOL_WIDTH), tile(d),
            pl.BlockSpec((rows, 3 * d), const),
            pl.BlockSpec(pool_lin.shape, lambda b, t: (0, 0, 0)),
            pl.BlockSpec((1, POOL_WIDTH), const),
            pl.BlockSpec(wo_pool.shape, const),
            pl.BlockSpec((1, d), const),
        ],
        out_specs=tile(d),
        scratch_shapes=[pltpu.VMEM((tm + 2 * POOL_HALO, POOL_WIDTH), F32)],
        compiler_params=_params(("arbitrary", "arbitrary")),
        name="pool_out",
    )(u, u, u, sg, x1, mod1, pool_lin, pool_scale, wo_pool, final_norm)


def _rope_tables(n, nctx):
    rows = n // GRID_W
    row = jnp.repeat(jnp.arange(rows, dtype=F32), GRID_W)
    col = jnp.tile(jnp.arange(GRID_W, dtype=F32), rows)
    axis_dim = MLA_ROPE // 2
    inv_freq = 1.0 / (ROPE_BASE ** (jnp.arange(0, axis_dim, 2, dtype=F32) / axis_dim))
    ang = jnp.concatenate([row[:, None] * inv_freq, col[:, None] * inv_freq], axis=-1)
    cos, sin = jnp.cos(ang), jnp.sin(ang)
    half = MLA_ROPE // 2
    cos_all = jnp.concatenate([jnp.ones((nctx, half), F32), cos], axis=0)
    sin_all = jnp.concatenate([jnp.zeros((nctx, half), F32), sin], axis=0)
    left = jnp.zeros((nctx + n, MLA_NOPE), F32)
    right = jnp.zeros((nctx + n, LANES - MLA_NOPE - MLA_ROPE), F32)
    cos_k = jnp.concatenate([left, cos_all, cos_all, right], axis=1)
    sin_k = jnp.concatenate([left, -sin_all, sin_all, right], axis=1)
    return cos.T, sin.T, cos_k, sin_k


def _pad_row(v):
    v = v.reshape(1, -1).astype(F32)
    return jnp.pad(v, ((0, 0), (0, LANES - v.shape[1])))


def kernel(x, c, ctx, c_ctx, mod_w, mod_b, norm_w, w_in_mix, q_norm, w_uq, kv_norm, w_ukv, conv_w, conv_b,
           a_log, dt_bias, d_skip, ssd_norm, w_out_mix, w_in_pool, pool_lin, pool_scale, w_out_pool, final_norm):
    bsz, n, d = x.shape
    nctx = ctx.shape[1]
    tm = 256
    tm_out = 512
    assert d == D_MODEL and n % tm_out == 0 and nctx % tm == 0 and n % GRID_W == 0
    assert mod_w.shape[0] == 2, "two layers: one mixing layer followed by one pooling layer"

    rows = -(-(bsz + 1) // SUBLANES) * SUBLANES
    cvec = jnp.zeros((rows, d), F32).at[:bsz].set(c).at[bsz].set(c_ctx)
    mods = _modulation(cvec, mod_w, mod_b)
    mod0, mod1 = mods[0], mods[1]

    w = w_in_mix[0]
    offs = np.cumsum((0,) + MIX_SPLITS)
    w_qa, w_kva, w_kpe, w_ga, w_z, w_xbc, w_dt = [w[:, offs[i]:offs[i + 1]] for i in range(len(MIX_SPLITS))]
    half = MLA_ROPE // 2
    kpe_pad = jnp.zeros((d, LANES), F32).at[:, MLA_NOPE:MLA_NOPE + MLA_ROPE].set(w_kpe)
    kpe_swap = jnp.concatenate([w_kpe[:, half:], w_kpe[:, :half]], axis=1)
    kps_pad = jnp.zeros((d, LANES), F32).at[:, MLA_NOPE:MLA_NOPE + MLA_ROPE].set(kpe_swap)
    dt_pad = jnp.pad(jnp.tile(w_dt, (1, DT_REPL)), ((0, 0), (0, LANES - DT_REPL * DT_COLS)))
    wn = jnp.concatenate([w_qa, w_kva, kpe_pad, kps_pad, w_z, w_xbc, dt_pad], axis=1).astype(BF16)
    wg_t = w_ga.T.astype(BF16)
    wuq = w_uq[0].reshape(MLA_Q_RANK, MLA_HEADS, MLA_NOPE + MLA_ROPE)
    wuq = jnp.pad(wuq, ((0, 0), (0, 0), (0, HEAD_PAD - MLA_NOPE - MLA_ROPE)))
    wuq_t = wuq.reshape(MLA_Q_RANK, MLA_HEADS * HEAD_PAD).T.astype(BF16)
    wukv = w_ukv[0].reshape(MLA_KV_RANK, MLA_HEADS, MLA_NOPE + MLA_V)
    wuk_p = jnp.pad(wukv[:, :, :MLA_NOPE], ((0, 0), (0, 0), (0, HEAD_PAD - MLA_NOPE)))
    wuk_p = wuk_p.reshape(MLA_KV_RANK, MLA_HEADS * HEAD_PAD).astype(BF16)
    wuv_t = wukv[:, :, MLA_NOPE:].reshape(MLA_KV_RANK, MLA_HEADS * MLA_V).T.astype(BF16)
    cos_q, sin_q, cos_k, sin_k = _rope_tables(n, nctx)

    q_t, kcat, v_t, g_t, sz, u_ssd, dt_raw = _mix_in(
        x, ctx, mod0, norm_w[0:1], wn, wg_t, q_norm[0:1], kv_norm[0:1], wuq_t, wuk_p, wuv_t,
        cos_q, sin_q, cos_k, sin_k, conv_w[0], conv_b[0:1], tm)

    a_t = _attention(q_t, kcat, v_t, g_t)

    yf, yb = _ssd(u_ssd, dt_raw, _pad_row(jnp.tile(a_log[0].reshape(-1), DT_REPL)),
                  _pad_row(jnp.tile(dt_bias[0].reshape(-1), DT_REPL)),
                  jnp.repeat(d_skip[0].astype(F32), SSD_HEAD_DIM, axis=1), nctx)

    wo = w_out_mix[0].astype(BF16)
    x1, u, sg = _mix_out(a_t, yf, yb, sz, x, mod0, mod1, ssd_norm[0:1], wo[:MLA_WIDTH], wo[MLA_WIDTH:],
                         norm_w[1:2], w_in_pool[0].astype(BF16), tm_out)

    return _pool_out(u, sg, x1, mod1, pool_lin[0].astype(BF16), pool_scale[0:1], w_out_pool[0].astype(BF16),
                     final_norm.reshape(1, d), tm_out)
```

```python
import functools
import math

import jax
import jax.numpy as jnp
import numpy as np
from jax import lax
from jax.experimental import pallas as pl
from jax.experimental.pallas import tpu as pltpu

D_MODEL = 1024
GRID_W = 64

MLA_HEADS = 8
MLA_NOPE = 64
MLA_ROPE = 32
MLA_V = 64
MLA_Q_RANK = 384
MLA_KV_RANK = 256
MLA_WIDTH = MLA_HEADS * MLA_V
ROPE_BASE = 10000.0
HEAD_PAD = 128
V_ROWS = 80

SSD_HEADS = 8
SSD_HEAD_DIM = 64
SSD_WIDTH = SSD_HEADS * SSD_HEAD_DIM
SSD_GROUPS = 2
SSD_STATE = 64
SSD_CONV = 3
SSD_CHUNK = 128
SSD_CONV_DIM = SSD_WIDTH + 2 * SSD_GROUPS * SSD_STATE
MIX_SPLITS = (MLA_Q_RANK, MLA_KV_RANK, MLA_ROPE, MLA_WIDTH, SSD_WIDTH, SSD_CONV_DIM, 2 * SSD_HEADS)
SSD_STEP_CHUNKS = 2
DT_COLS = 2 * SSD_HEADS
DT_REPL = 3

POOL_WINDOWS = (2, 4, 8, 16)
POOL_GROUPS = len(POOL_WINDOWS)
POOL_WIDTH = D_MODEL
POOL_GROUP_DIM = POOL_WIDTH // POOL_GROUPS
POOL_HALO = 8

RMS_EPS = 1e-6
LANES = 128
SUBLANES = 8
VMEM_LIMIT = 56 * 1024 * 1024

C_QA = 0
C_KVA = C_QA + MLA_Q_RANK
C_KPE = C_KVA + MLA_KV_RANK
C_KPS = C_KPE + LANES
C_Z = C_KPS + LANES
C_XBC = C_Z + SSD_WIDTH
C_DT = C_XBC + SSD_CONV_DIM
C_END = C_DT + LANES

BF16 = jnp.bfloat16
F32 = jnp.float32
NEG_BIG = -1e30


def _rms(x, w):
    y = x * lax.rsqrt(jnp.mean(x * x, axis=-1, keepdims=True) + RMS_EPS)
    return y * w


def _silu(x):
    return x * (1.0 / (1.0 + jnp.exp(-x)))


def _softplus(x):
    return jnp.maximum(x, 0.0) + jnp.log(1.0 + jnp.exp(-jnp.abs(x)))


def _dot(a, b):
    return jnp.dot(a, b, preferred_element_type=F32)


def _dot_nt(a, b):
    return lax.dot_general(a, b, (((1,), (1,)), ((), ())), preferred_element_type=F32)


def _dot_tn(a, b):
    return lax.dot_general(a, b, (((0,), (0,)), ((), ())), preferred_element_type=F32)


def _params(sem):
    return pltpu.CompilerParams(dimension_semantics=sem, vmem_limit_bytes=VMEM_LIMIT)


def _mod_kernel(c_ref, w_ref, b_ref, o_ref):
    sc = _silu(c_ref[...]).astype(BF16)
    o_ref[0] = _dot(sc, w_ref[0].astype(BF16)) + b_ref[0]


def _modulation(cvec, mod_w, mod_b):
    depth, d, d3 = mod_w.shape
    rows = cvec.shape[0]
    nblk = d3 // d
    return pl.pallas_call(
        _mod_kernel,
        out_shape=jax.ShapeDtypeStruct((depth, rows, d3), F32),
        grid=(depth, nblk),
        in_specs=[
            pl.BlockSpec((rows, d), lambda i, j: (0, 0)),
            pl.BlockSpec((1, d, d), lambda i, j: (i, 0, j)),
            pl.BlockSpec((1, 1, d), lambda i, j: (i, 0, j)),
        ],
        out_specs=pl.BlockSpec((1, rows, d), lambda i, j: (i, 0, j)),
        compiler_params=_params(("arbitrary", "arbitrary")),
        name="modulation",
    )(cvec, mod_w, mod_b.reshape(depth, 1, d3))


def _mix_in_kernel(nct, nt, nbatch, x_ref, ctx_ref, mod_ref, nw_ref, wn_ref, wg_ref, qn_ref, kvn_ref,
                   wuq_ref, wuk_ref, wuv_ref, cosq_ref, sinq_ref, cosk_ref, sink_ref, cw_ref, cb_ref,
                   q_ref, k_ref, v_ref, g_ref, sz_ref, u_ref, dt_ref, xbuf_ref, last_ref):
    b = pl.program_id(0)
    t = pl.program_id(1)

    @pl.when(t == 0)
    def _():
        last_ref[...] = jnp.zeros_like(last_ref)
        xbuf_ref[...] = jnp.zeros_like(xbuf_ref)

    def finish_conv(xbc_new):
        xc = xbuf_ref[(t + 1) % 2]
        xbuf_ref[t % 2] = xbc_new
        tp = t - 1
        tm = xc.shape[0]
        first_seg = jnp.logical_or(tp == 0, tp == nct)
        last_seg = jnp.logical_or(tp == nct - 1, tp == nt - 1)
        prev_row = jnp.where(first_seg, 0.0, last_ref[0:1, :])
        next_row = jnp.where(last_seg, 0.0, xbc_new[0:1, :])
        ridx = lax.broadcasted_iota(jnp.int32, (tm, 1), 0)
        x_m1 = jnp.where(ridx == 0, prev_row, pltpu.roll(xc, 1, axis=0))
        x_p1 = jnp.where(ridx == tm - 1, next_row, pltpu.roll(xc, tm - 1, axis=0))
        cw = cw_ref[...]
        u_ref[0] = _silu(cb_ref[...] + x_m1 * cw[0:1] + xc * cw[1:2] + x_p1 * cw[2:3])
        last_ref[0:1, :] = xc[tm - 1:tm]

    _mix_in_project(nct, nbatch, b, jnp.minimum(t, nt - 1), x_ref, ctx_ref, mod_ref, nw_ref, wn_ref,
                    wg_ref, qn_ref, kvn_ref, wuq_ref, wuk_ref, wuv_ref, cosq_ref, sinq_ref,
                    cosk_ref, sink_ref, q_ref, k_ref, v_ref, g_ref, sz_ref, dt_ref, finish_conv)


def _mix_in_project(nct, nbatch, b, t, x_ref, ctx_ref, mod_ref, nw_ref, wn_ref, wg_ref, qn_ref, kvn_ref,
                    wuq_ref, wuk_ref, wuv_ref, cosq_ref, sinq_ref, cosk_ref, sink_ref,
                    q_ref, k_ref, v_ref, g_ref, sz_ref, dt_ref, finish_conv):
    is_ctx = t < nct
    xin = jnp.where(is_ctx, ctx_ref[0], x_ref[0])
    row = jnp.where(is_ctx, nbatch, b)
    mod = mod_ref[pl.ds(row, 1), :]
    shift = mod[:, 0:D_MODEL]
    scale = mod[:, D_MODEL:2 * D_MODEL]
    h = _rms(xin, nw_ref[...]) * (1.0 + scale) + shift
    hb = h.astype(BF16)
    proj = _dot(hb, wn_ref[...])

    dt_ref[0] = proj[:, C_DT:C_END]
    finish_conv(proj[:, C_XBC:C_DT])

    ckv = _rms(proj[:, C_KVA:C_KPE], kvn_ref[...]).astype(BF16)
    kpad = _dot(ckv, wuk_ref[...])
    kpe = proj[:, C_KPE:C_KPS] * cosk_ref[...] + proj[:, C_KPS:C_Z] * sink_ref[...]
    vt = _dot_nt(wuv_ref[...], ckv)
    tm = vt.shape[1]
    ones_rows = (lax.broadcasted_iota(jnp.int32, (V_ROWS - MLA_V, tm), 0) == 0).astype(BF16)
    for hd in range(MLA_HEADS):
        k_ref[0, hd] = (kpad[:, hd * HEAD_PAD:(hd + 1) * HEAD_PAD] + kpe).astype(BF16)
        v_ref[0, hd, 0, 0:MLA_V, :] = vt[hd * MLA_V:(hd + 1) * MLA_V].astype(BF16)
        v_ref[0, hd, 0, MLA_V:V_ROWS, :] = ones_rows

    @pl.when(jnp.logical_not(is_ctx))
    def _():
        sz_ref[0] = _silu(proj[:, C_Z:C_XBC])
        g_ref[0] = _silu(_dot_nt(wg_ref[...], hb))
        cq = _rms(proj[:, C_QA:C_KVA], qn_ref[...]).astype(BF16)
        qt = _dot_nt(wuq_ref[...], cq)
        sm_scale = (MLA_NOPE + MLA_ROPE) ** -0.5 * math.log2(math.e)
        cos = cosq_ref[...]
        sin = sinq_ref[...]
        half = MLA_ROPE // 2
        for hd in range(MLA_HEADS):
            base = hd * HEAD_PAD
            t1 = qt[base + MLA_NOPE:base + MLA_NOPE + half]
            t2 = qt[base + MLA_NOPE + half:base + MLA_NOPE + MLA_ROPE]
            q_ref[0, hd, 0, 0:MLA_NOPE, :] = (qt[base:base + MLA_NOPE] * sm_scale).astype(BF16)
            q_ref[0, hd, 0, MLA_NOPE:MLA_NOPE + half, :] = ((t1 * cos - t2 * sin) * sm_scale).astype(BF16)
            q_ref[0, hd, 0, MLA_NOPE + half:MLA_NOPE + MLA_ROPE, :] = ((t2 * cos + t1 * sin) * sm_scale).astype(BF16)
            q_ref[0, hd, 0, MLA_NOPE + MLA_ROPE:HEAD_PAD, :] = jnp.zeros((HEAD_PAD - MLA_NOPE - MLA_ROPE, tm), BF16)


def _mix_in(x, ctx, mod0, norm_w, wn, wg_t, q_norm, kv_norm, wuq_t, wuk_p, wuv_t,
            cos_q, sin_q, cos_k, sin_k, conv_w, conv_b, tm):
    bsz, n, d = x.shape
    nctx = ctx.shape[1]
    nct = nctx // tm
    nlt = n // tm
    nt = nct + nlt
    tt = nctx + n
    rows = mod0.shape[0]

    def lat(t):
        return jnp.clip(t - nct, 0, nlt - 1)

    def cur(t):
        return jnp.minimum(t, nt - 1)

    const = lambda b, t: (0, 0)
    out_shape = (
        jax.ShapeDtypeStruct((bsz, MLA_HEADS, n // tm, HEAD_PAD, tm), BF16),
        jax.ShapeDtypeStruct((bsz, MLA_HEADS, tt, HEAD_PAD), BF16),
        jax.ShapeDtypeStruct((bsz, MLA_HEADS, tt // tm, V_ROWS, tm), BF16),
        jax.ShapeDtypeStruct((bsz, MLA_WIDTH, n), F32),
        jax.ShapeDtypeStruct((bsz, n, SSD_WIDTH), F32),
        jax.ShapeDtypeStruct((bsz, tt, SSD_CONV_DIM), F32),
        jax.ShapeDtypeStruct((bsz, tt, LANES), F32),
    )
    out_specs = (
        pl.BlockSpec((1, MLA_HEADS, 1, HEAD_PAD, tm), lambda b, t: (b, 0, lat(t), 0, 0)),
        pl.BlockSpec((1, MLA_HEADS, tm, HEAD_PAD), lambda b, t: (b, 0, cur(t), 0)),
        pl.BlockSpec((1, MLA_HEADS, 1, V_ROWS, tm), lambda b, t: (b, 0, cur(t), 0, 0)),
        pl.BlockSpec((1, MLA_WIDTH, tm), lambda b, t: (b, 0, lat(t))),
        pl.BlockSpec((1, tm, SSD_WIDTH), lambda b, t: (b, lat(t), 0)),
        pl.BlockSpec((1, tm, SSD_CONV_DIM), lambda b, t: (b, jnp.maximum(t - 1, 0), 0)),
        pl.BlockSpec((1, tm, LANES), lambda b, t: (b, cur(t), 0)),
    )
    in_specs = [
        pl.BlockSpec((1, tm, d), lambda b, t: (b, lat(t), 0)),
        pl.BlockSpec((1, tm, d), lambda b, t: (b, jnp.minimum(t, nct - 1), 0)),
        pl.BlockSpec((rows, 3 * d), const),
        pl.BlockSpec((1, d), const),
        pl.BlockSpec(wn.shape, const),
        pl.BlockSpec(wg_t.shape, const),
        pl.BlockSpec((1, MLA_Q_RANK), const),
        pl.BlockSpec((1, MLA_KV_RANK), const),
        pl.BlockSpec(wuq_t.shape, const),
        pl.BlockSpec(wuk_p.shape, const),
        pl.BlockSpec(wuv_t.shape, const),
        pl.BlockSpec((MLA_ROPE // 2, tm), lambda b, t: (0, lat(t))),
        pl.BlockSpec((MLA_ROPE // 2, tm), lambda b, t: (0, lat(t))),
        pl.BlockSpec((tm, LANES), lambda b, t: (cur(t), 0)),
        pl.BlockSpec((tm, LANES), lambda b, t: (cur(t), 0)),
        pl.BlockSpec(conv_w.shape, const),
        pl.BlockSpec(conv_b.shape, const),
    ]
    return pl.pallas_call(
        functools.partial(_mix_in_kernel, nct, nt, bsz),
        out_shape=out_shape,
        grid=(bsz, nt + 1),
        in_specs=in_specs,
        out_specs=out_specs,
        scratch_shapes=[pltpu.VMEM((2, tm, SSD_CONV_DIM), F32), pltpu.VMEM((SUBLANES, SSD_CONV_DIM), F32)],
        compiler_params=_params(("arbitrary", "arbitrary")),
        name="mix_in",
    )(x, ctx, mod0, norm_w, wn, wg_t, q_norm, kv_norm, wuq_t, wuk_p, wuv_t, cos_q, sin_q, cos_k, sin_k,
      conv_w, conv_b)


def _attn_kernel(nkb, sub, kt, ntiles, q_ref, k_ref, v_ref, g_ref, qn_ref, kn_ref, o_ref, sa_ref, sb_ref, ma_ref):
    i = pl.program_id(2)
    first_step = jnp.logical_and(jnp.logical_and(pl.program_id(0) == 0, pl.program_id(1) == 0), i == 0)
    tq = 2 * q_ref.shape[4]
    tps = o_ref.shape[2] // tq
    last_of_head = i == ntiles // tps - 1
    kb = sub * kt

    def q_tile(t):
        return jnp.concatenate([q_ref[0, 0, 2 * t], q_ref[0, 0, 2 * t + 1]], axis=1)

    def keys(j, look_ahead):
        rows = slice(j * kb, (j + 1) * kb)
        if look_ahead:
            return jnp.where(last_of_head, kn_ref[0, 0, rows, :], k_ref[0, 0, rows, :])
        return k_ref[0, 0, rows, :]

    def scores(j, q, s_ref, look_ahead=False):
        s = _dot(keys(j, look_ahead), q)
        s_ref[j] = s
        while s.shape[0] > SUBLANES and s.shape[0] % (2 * SUBLANES) == 0:
            half = s.shape[0] // 2
            s = jnp.maximum(s[:half], s[half:])
        return jnp.max(s, axis=0, keepdims=True)

    def weighted_values(j, s_ref, m, acc):
        p = jnp.exp2(s_ref[j] - m).astype(BF16)
        for r in range(sub):
            acc = acc + _dot(v_ref[0, 0, j * sub + r], p[r * kt:(r + 1) * kt])
        return acc

    def fused(q_next, w_ref, r_ref, m_read, look_ahead=False):
        acc = jnp.zeros((V_ROWS, tq), F32)
        m_next = jnp.full((1, tq), NEG_BIG, F32)
        for j in range(nkb):
            m_next = jnp.maximum(m_next, scores(j, q_next, w_ref, look_ahead))
            acc = weighted_values(j, r_ref, m_read, acc)
        return acc, m_next

    def finish(acc, lo):
        out = acc[0:MLA_V] / acc[MLA_V:MLA_V + 1]
        o_ref[0, :, lo:lo + tq] = (out * g_ref[0, :, lo:lo + tq]).astype(BF16)

    @pl.when(first_step)
    def _():
        q0 = q_tile(0)
        m0 = jnp.full((1, tq), NEG_BIG, F32)
        for j in range(nkb):
            m0 = jnp.maximum(m0, scores(j, q0, sa_ref))
        ma_ref[0:1, :] = m0

    bufs = (sa_ref, sb_ref)
    m_read = ma_ref[0:1, :]
    for k in range(tps):
        if k < tps - 1:
            q_next, ahead = q_tile(tps * i + k + 1), False
        else:
            q_ahead = jnp.concatenate([qn_ref[0, 0, 0], qn_ref[0, 0, 1]], axis=1)
            q_next = jnp.where(last_of_head, q_ahead, q_tile(jnp.minimum(tps * i + tps, ntiles - 1)))
            ahead = True
        acc, m_read = fused(q_next, bufs[(k + 1) % 2], bufs[k % 2], m_read, look_ahead=ahead)
        finish(acc, k * tq)
    ma_ref[0:1, :] = m_read


def _attention(q_t, kcat, v_t, g_t):
    bsz, nh, nqt, _, qt = q_t.shape
    tq = 2 * qt
    n = nqt * qt
    ntiles = n // tq
    tt = kcat.shape[2]
    nvb, kt = v_t.shape[2], v_t.shape[4]
    sub = 1
    nkb = nvb // sub
    tps = 4 if ntiles % 4 == 0 else 2
    assert ntiles % tps == 0

    def nxt(b, h):
        f = jnp.minimum(b * nh + h + 1, bsz * nh - 1)
        return f // nh, f % nh

    return pl.pallas_call(
        functools.partial(_attn_kernel, nkb, sub, kt, ntiles),
        out_shape=jax.ShapeDtypeStruct((bsz, nh * MLA_V, n), BF16),
        grid=(bsz, nh, ntiles // tps),
        in_specs=[
            pl.BlockSpec((1, 1, nqt, HEAD_PAD, qt), lambda b, h, i: (b, h, 0, 0, 0)),
            pl.BlockSpec((1, 1, tt, HEAD_PAD), lambda b, h, i: (b, h, 0, 0)),
            pl.BlockSpec((1, 1, nvb, V_ROWS, kt), lambda b, h, i: (b, h, 0, 0, 0)),
            pl.BlockSpec((1, MLA_V, tps * tq), lambda b, h, i: (b, h, i)),
            pl.BlockSpec((1, 1, 2, HEAD_PAD, qt), lambda b, h, i: (*nxt(b, h), 0, 0, 0)),
            pl.BlockSpec((1, 1, tt, HEAD_PAD), lambda b, h, i: (*nxt(b, h), 0, 0)),
        ],
        out_specs=pl.BlockSpec((1, MLA_V, tps * tq), lambda b, h, i: (b, h, i)),
        scratch_shapes=[pltpu.VMEM((nkb, sub * kt, tq), F32), pltpu.VMEM((nkb, sub * kt, tq), F32),
                        pltpu.VMEM((SUBLANES, tq), F32)],
        compiler_params=_params(("arbitrary", "arbitrary", "arbitrary")),
        name="attention",
    )(q_t, kcat, v_t, g_t, q_t, kcat)


def _ssd_direction(d, u, dtm, acs, skip_ref, eh_ref, ew_ref, state_ref, y_ref, y_row):
    L = SSD_CHUNK
    xs = u[:, 0:SSD_WIDTH]
    gn = SSD_GROUPS * SSD_STATE
    bm = u[:, SSD_WIDTH:SSD_WIDTH + gn].astype(BF16)
    cm = u[:, SSD_WIDTH + gn:SSD_WIDTH + 2 * gn].astype(BF16)

    e_head = eh_ref[d]
    e_wide = ew_ref[d]
    dt_x = _dot(_split3(dtm), e_head)
    acs3 = _split3(acs)
    acs_x = _dot(acs3, e_head)
    acs_t = acs.T
    end = L - 1 if d == 0 else 0
    a_end = acs_x[end:end + 1, :]

    li = lax.broadcasted_iota(jnp.int32, (L, L), 0)
    si = lax.broadcasted_iota(jnp.int32, (L, L), 1)
    mask = (si <= li) if d == 0 else (si >= li)
    lane = lax.broadcasted_iota(jnp.int32, (L, LANES), 1)
    low = lane < SSD_STATE
    zero_b = jnp.zeros((L, LANES), BF16)

    xd = xs * dt_x
    xdb = xd.astype(BF16)
    heads_per_group = SSD_HEADS // SSD_GROUPS
    ys = []
    for g in range(SSD_GROUPS):
        cg = jnp.where(low, cm, zero_b) if g == 0 else jnp.where(low, zero_b, cm)
        cbm = _dot_nt(cg, bm)
        for pr in range(heads_per_group // 2):
            ms = []
            h0 = g * heads_per_group + 2 * pr
            acs_w = _dot(acs3, e_wide[:, h0 * L:(h0 + 2) * L])
            for r in range(2):
                hd = h0 + r
                col = d * SSD_HEADS + hd
                diff = acs_w[:, r * L:(r + 1) * L] - acs_t[col:col + 1, :]
                lmat = jnp.exp(jnp.where(mask, diff, -jnp.inf))
                ms.append((cbm * lmat).astype(BF16))
            pair = (g * heads_per_group) // 2 + pr
            xp = xdb[:, pair * LANES:(pair + 1) * LANES]
            rhs = jnp.concatenate([jnp.where(low, xp, zero_b), jnp.where(low, zero_b, xp)], axis=0)
            ys.append(_dot(jnp.concatenate(ms, axis=1), rhs))
    h_in = state_ref[d]
    y = jnp.concatenate(ys, axis=1) + jnp.exp(acs_x) * _dot(cm, h_in.astype(BF16)) + skip_ref[d:d + 1, :] * xs
    y_ref[0, y_row:y_row + L, :] = y

    xdw = (xd * jnp.exp(a_end - acs_x)).astype(BF16)
    st = _dot_tn(bm, xdw)
    gi = lax.broadcasted_iota(jnp.int32, st.shape, 0) // SSD_STATE
    hi = lax.broadcasted_iota(jnp.int32, st.shape, 1) // (heads_per_group * SSD_HEAD_DIM)
    state_ref[d] = h_in * jnp.exp(a_end) + jnp.where(gi == hi, st, 0.0)


def _split3(x):
    lane = lax.broadcasted_iota(jnp.int32, x.shape, 1)
    r1 = x - x.astype(BF16).astype(F32)
    r2 = r1 - r1.astype(BF16).astype(F32)
    return jnp.where(lane < DT_COLS, x, jnp.where(lane < 2 * DT_COLS, r1, r2)).astype(BF16)


def _ssd_kernel(uf_ref, dtf_ref, ub_ref, dtb_raw_ref, alog_ref, dtbias_ref, skip_ref, tri_ref, eh_ref, ew_ref,
                yf_ref, yb_ref, state_ref):
    s = pl.program_id(1)
    L = SSD_CHUNK
    nsub = SSD_STEP_CHUNKS

    @pl.when(s == 0)
    def _():
        state_ref[...] = jnp.zeros_like(state_ref)

    neg_a = -jnp.exp(alog_ref[...])
    dtm_f = _softplus(dtf_ref[0] + dtbias_ref[...])
    dtm_b = _softplus(dtb_raw_ref[0] + dtbias_ref[...])
    a_f = dtm_f * neg_a
    a_b = dtm_b * neg_a
    rows = [slice(c * L, (c + 1) * L) for c in range(nsub)]
    a_all = jnp.concatenate([a_f[r] for r in rows] + [a_b[r] for r in rows], axis=1)
    a_hi = a_all.astype(BF16)
    r1 = a_all - a_hi.astype(F32)
    a_mid = r1.astype(BF16)
    a_lo = (r1 - a_mid.astype(F32)).astype(BF16)
    tri = tri_ref[...]
    cs = _dot(tri, a_hi) + (_dot(tri, a_mid) + _dot(tri, a_lo))
    uf = uf_ref[0]
    ub = ub_ref[0]
    for c in range(nsub):
        cf, cbk = c, nsub - 1 - c
        acs_f = cs[:, cf * LANES:(cf + 1) * LANES]
        cs_b = cs[:, (nsub + cbk) * LANES:(nsub + cbk + 1) * LANES]
        acs_b = cs_b[L - 1:L, :] - cs_b + a_b[rows[cbk]]
        _ssd_direction(0, uf[rows[cf]], dtm_f[rows[cf]], acs_f, skip_ref, eh_ref, ew_ref, state_ref, yf_ref, cf * L)
        _ssd_direction(1, ub[rows[cbk]], dtm_b[rows[cbk]], acs_b, skip_ref, eh_ref, ew_ref, state_ref, yb_ref,
                       cbk * L)


def _ssd_constants():
    L = SSD_CHUNK
    tri = np.tril(np.ones((L, L), np.float32))
    e_head = np.zeros((2, LANES, SSD_WIDTH), np.float32)
    e_wide = np.zeros((2, LANES, SSD_HEADS * L), np.float32)
    for d in range(2):
        for rep in range(DT_REPL):
            for h in range(SSD_HEADS):
                r = rep * DT_COLS + d * SSD_HEADS + h
                e_head[d, r, h * SSD_HEAD_DIM:(h + 1) * SSD_HEAD_DIM] = 1.0
                e_wide[d, r, h * L:(h + 1) * L] = 1.0
    return jnp.asarray(tri, BF16), jnp.asarray(e_head, BF16), jnp.asarray(e_wide, BF16)


def _ssd(u, dt_raw, alog_row, dtbias_row, skip_x, nctx):
    bsz, tt, cd = u.shape
    tri, e_head, e_wide = _ssd_constants()
    L = SSD_STEP_CHUNKS * SSD_CHUNK
    assert nctx % L == 0 and tt % L == 0
    nchunks = tt // L
    ncc = nctx // L
    n = tt - nctx

    def cf(s):
        return s

    def cb(s):
        return jnp.where(s < ncc, ncc - 1 - s, nchunks - 1 - (s - ncc))

    def specs(cfn):
        return [
            pl.BlockSpec((1, L, cd), lambda b, s: (b, cfn(s), 0)),
            pl.BlockSpec((1, L, LANES), lambda b, s: (b, cfn(s), 0)),
        ]

    const = lambda b, s: (0, 0)
    in_specs = specs(cf) + specs(cb) + [
        pl.BlockSpec((1, LANES), const),
        pl.BlockSpec((1, LANES), const),
        pl.BlockSpec(skip_x.shape, const),
        pl.BlockSpec(tri.shape, const),
        pl.BlockSpec(e_head.shape, lambda b, s: (0, 0, 0)),
        pl.BlockSpec(e_wide.shape, lambda b, s: (0, 0, 0)),
    ]
    out_specs = (
        pl.BlockSpec((1, L, SSD_WIDTH), lambda b, s: (b, jnp.maximum(cf(s) - ncc, 0), 0)),
        pl.BlockSpec((1, L, SSD_WIDTH), lambda b, s: (b, nchunks - 1 - ncc - jnp.maximum(s - ncc, 0), 0)),
    )
    y_shape = jax.ShapeDtypeStruct((bsz, n, SSD_WIDTH), F32)
    return pl.pallas_call(
        _ssd_kernel,
        out_shape=(y_shape, y_shape),
        grid=(bsz, nchunks),
        in_specs=in_specs,
        out_specs=out_specs,
        scratch_shapes=[pltpu.VMEM((2, SSD_GROUPS * SSD_STATE, SSD_WIDTH), F32)],
        compiler_params=_params(("arbitrary", "arbitrary")),
        name="ssd",
    )(u, dt_raw, u, dt_raw, alog_row, dtbias_row, skip_x, tri, e_head, e_wide)


def _mix_out_kernel(nbatch, a_ref, yf_ref, yb_ref, sz_ref, x_ref, mod0_ref, mod1_ref, sn_ref, woa_ref, wos_ref,
                    nw1_ref, wp_ref, x1_ref, u_ref, sg_ref):
    b = pl.program_id(0)
    y = (yf_ref[0] + yb_ref[0]) * sz_ref[0]
    ssd = _rms(y, sn_ref[...]).astype(BF16)
    o = _dot_tn(a_ref[0], woa_ref[...]) + _dot(ssd, wos_ref[...])
    gate0 = mod0_ref[pl.ds(b, 1), 2 * D_MODEL:3 * D_MODEL]
    x1 = x_ref[0] + gate0 * o
    x1_ref[0] = x1
    mod1 = mod1_ref[pl.ds(b, 1), :]
    h1 = _rms(x1, nw1_ref[...]) * (1.0 + mod1[:, D_MODEL:2 * D_MODEL]) + mod1[:, 0:D_MODEL]
    ug = _dot(h1.astype(BF16), wp_ref[...])
    u_ref[0] = ug[:, 0:POOL_WIDTH]
    sg_ref[0] = _silu(ug[:, POOL_WIDTH:2 * POOL_WIDTH])


def _mix_out(a_t, yf, yb, sz, x, mod0, mod1, ssd_norm, wo_a, wo_s, norm_w1, w_pool, tm):
    bsz, n, d = x.shape
    rows = mod0.shape[0]
    const = lambda b, t: (0, 0)
    tile = lambda w: pl.BlockSpec((1, tm, w), lambda b, t: (b, t, 0))
    return pl.pallas_call(
        functools.partial(_mix_out_kernel, bsz),
        out_shape=(jax.ShapeDtypeStruct((bsz, n, d), F32),
                   jax.ShapeDtypeStruct((bsz, n, POOL_WIDTH), F32),
                   jax.ShapeDtypeStruct((bsz, n, POOL_WIDTH), F32)),
        grid=(bsz, n // tm),
        in_specs=[
            pl.BlockSpec((1, MLA_WIDTH, tm), lambda b, t: (b, 0, t)),
            tile(SSD_WIDTH), tile(SSD_WIDTH), tile(SSD_WIDTH), tile(d),
            pl.BlockSpec((rows, 3 * d), const),
            pl.BlockSpec((rows, 3 * d), const),
            pl.BlockSpec((1, SSD_WIDTH), const),
            pl.BlockSpec(wo_a.shape, const),
            pl.BlockSpec(wo_s.shape, const),
            pl.BlockSpec((1, d), const),
            pl.BlockSpec(w_pool.shape, const),
        ],
        out_specs=(tile(d), tile(POOL_WIDTH), tile(POOL_WIDTH)),
        compiler_params=_params(("arbitrary", "arbitrary")),
        name="mix_out",
    )(a_t, yf, yb, sz, x, mod0, mod1, ssd_norm, wo_a, wo_s, norm_w1, w_pool)


def _pool_kernel(n, u_ref, up_ref, un_ref, sg_ref, x1_ref, mod1_ref, lin_ref, ps_ref, wo_ref, fn_ref,
                 o_ref, ext_ref):
    b = pl.program_id(0)
    t = pl.program_id(1)
    nt = pl.num_programs(1)
    tm = u_ref.shape[1]
    H = POOL_HALO
    ext_ref[0:H, :] = jnp.where(t == 0, 0.0, up_ref[0])
    ext_ref[H:H + tm, :] = u_ref[0]
    ext_ref[H + tm:H + tm + H, :] = jnp.where(t == nt - 1, 0.0, un_ref[0])

    tok = t * tm + lax.broadcasted_iota(jnp.int32, (tm, 1), 0)
    rows = tm + 2 * H
    ys = []
    for g, w in enumerate(POOL_WINDOWS):
        lanes = slice(g * POOL_GROUP_DIM, (g + 1) * POOL_GROUP_DIM)
        e = ext_ref[:, lanes]
        win = e + pltpu.roll(e, 1, axis=0)
        width = 2
        while width < w:
            win = pltpu.roll(win, width // 2, axis=0) + pltpu.roll(win, rows - width // 2, axis=0)
            width *= 2
        acc = win[H:H + tm]
        lo = jnp.maximum(tok - w // 2, 0)
        hi = jnp.minimum(tok + (w - w // 2 - 1), n - 1)
        cnt = (hi - lo + 1).astype(F32)
        m = acc / cnt - u_ref[0, :, lanes]
        ys.append(_dot(m.astype(BF16), lin_ref[g]))
    y = jnp.concatenate(ys, axis=-1) * ps_ref[...] * sg_ref[0]
    o = _dot(y.astype(BF16), wo_ref[...])
    gate1 = mod1_ref[pl.ds(b, 1), 2 * D_MODEL:3 * D_MODEL]
    x2 = x1_ref[0] + gate1 * o
    o_ref[0] = _rms(x2, fn_ref[...])


def _pool_out(u, sg, x1, mod1, pool_lin, pool_scale, wo_pool, final_norm, tm):
    bsz, n, d = x1.shape
    rows = mod1.shape[0]
    r8 = tm // SUBLANES
    nrow8 = n // SUBLANES
    const = lambda b, t: (0, 0)
    tile = lambda w: pl.BlockSpec((1, tm, w), lambda b, t: (b, t, 0))
    return pl.pallas_call(
        functools.partial(_pool_kernel, n),
        out_shape=jax.ShapeDtypeStruct((bsz, n, d), F32),
        grid=(bsz, n // tm),
        in_specs=[
            tile(POOL_WIDTH),
            pl.BlockSpec((1, POOL_HALO, POOL_WIDTH), lambda b, t: (b, jnp.maximum(t * r8 - 1, 0), 0)),
            pl.BlockSpec((1, POOL_HALO, POOL_WIDTH), lambda b, t: (b, jnp.minimum((t + 1) * r8, nrow8 - 1), 0)),
            tile(POOL_WIDTH), tile(d),
            pl.BlockSpec((rows, 3 * d), const),
            pl.BlockSpec(pool_lin.shape, lambda b, t: (0, 0, 0)),
            pl.BlockSpec((1, POOL_WIDTH), const),
            pl.BlockSpec(wo_pool.shape, const),
            pl.BlockSpec((1, d), const),
        ],
        out_specs=tile(d),
        scratch_shapes=[pltpu.VMEM((tm + 2 * POOL_HALO, POOL_WIDTH), F32)],
        compiler_params=_params(("arbitrary", "arbitrary")),
        name="pool_out",
    )(u, u, u, sg, x1, mod1, pool_lin, pool_scale, wo_pool, final_norm)


def _rope_tables(n, nctx):
    rows = n // GRID_W
    row = jnp.repeat(jnp.arange(rows, dtype=F32), GRID_W)
    col = jnp.tile(jnp.arange(GRID_W, dtype=F32), rows)
    axis_dim = MLA_ROPE // 2
    inv_freq = 1.0 / (ROPE_BASE ** (jnp.arange(0, axis_dim, 2, dtype=F32) / axis_dim))
    ang = jnp.concatenate([row[:, None] * inv_freq, col[:, None] * inv_freq], axis=-1)
    cos, sin = jnp.cos(ang), jnp.sin(ang)
    half = MLA_ROPE // 2
    cos_all = jnp.concatenate([jnp.ones((nctx, half), F32), cos], axis=0)
    sin_all = jnp.concatenate([jnp.zeros((nctx, half), F32), sin], axis=0)
    left = jnp.zeros((nctx + n, MLA_NOPE), F32)
    right = jnp.zeros((nctx + n, LANES - MLA_NOPE - MLA_ROPE), F32)
    cos_k = jnp.concatenate([left, cos_all, cos_all, right], axis=1)
    sin_k = jnp.concatenate([left, -sin_all, sin_all, right], axis=1)
    return cos.T, sin.T, cos_k, sin_k


def _pad_row(v):
    v = v.reshape(1, -1).astype(F32)
    return jnp.pad(v, ((0, 0), (0, LANES - v.shape[1])))


def kernel(x, c, ctx, c_ctx, mod_w, mod_b, norm_w, w_in_mix, q_norm, w_uq, kv_norm, w_ukv, conv_w, conv_b,
           a_log, dt_bias, d_skip, ssd_norm, w_out_mix, w_in_pool, pool_lin, pool_scale, w_out_pool, final_norm):
    bsz, n, d = x.shape
    nctx = ctx.shape[1]
    tm = 256
    tm_out = 512
    assert d == D_MODEL and n % tm_out == 0 and nctx % tm == 0 and n % GRID_W == 0
    assert mod_w.shape[0] == 2, "two layers: one mixing layer followed by one pooling layer"

    rows = -(-(bsz + 1) // SUBLANES) * SUBLANES
    cvec = jnp.zeros((rows, d), F32).at[:bsz].set(c).at[bsz].set(c_ctx)
    mods = _modulation(cvec, mod_w, mod_b)
    mod0, mod1 = mods[0], mods[1]

    w = w_in_mix[0]
    offs = np.cumsum((0,) + MIX_SPLITS)
    w_qa, w_kva, w_kpe, w_ga, w_z, w_xbc, w_dt = [w[:, offs[i]:offs[i + 1]] for i in range(len(MIX_SPLITS))]
    half = MLA_ROPE // 2
    kpe_pad = jnp.zeros((d, LANES), F32).at[:, MLA_NOPE:MLA_NOPE + MLA_ROPE].set(w_kpe)
    kpe_swap = jnp.concatenate([w_kpe[:, half:], w_kpe[:, :half]], axis=1)
    kps_pad = jnp.zeros((d, LANES), F32).at[:, MLA_NOPE:MLA_NOPE + MLA_ROPE].set(kpe_swap)
    dt_pad = jnp.pad(jnp.tile(w_dt, (1, DT_REPL)), ((0, 0), (0, LANES - DT_REPL * DT_COLS)))
    wn = jnp.concatenate([w_qa, w_kva, kpe_pad, kps_pad, w_z, w_xbc, dt_pad], axis=1).astype(BF16)
    wg_t = w_ga.T.astype(BF16)
    wuq = w_uq[0].reshape(MLA_Q_RANK, MLA_HEADS, MLA_NOPE + MLA_ROPE)
    wuq = jnp.pad(wuq, ((0, 0), (0, 0), (0, HEAD_PAD - MLA_NOPE - MLA_ROPE)))
    wuq_t = wuq.reshape(MLA_Q_RANK, MLA_HEADS * HEAD_PAD).T.astype(BF16)
    wukv = w_ukv[0].reshape(MLA_KV_RANK, MLA_HEADS, MLA_NOPE + MLA_V)
    wuk_p = jnp.pad(wukv[:, :, :MLA_NOPE], ((0, 0), (0, 0), (0, HEAD_PAD - MLA_NOPE)))
    wuk_p = wuk_p.reshape(MLA_KV_RANK, MLA_HEADS * HEAD_PAD).astype(BF16)
    wuv_t = wukv[:, :, MLA_NOPE:].reshape(MLA_KV_RANK, MLA_HEADS * MLA_V).T.astype(BF16)
    cos_q, sin_q, cos_k, sin_k = _rope_tables(n, nctx)

    q_t, kcat, v_t, g_t, sz, u_ssd, dt_raw = _mix_in(
        x, ctx, mod0, norm_w[0:1], wn, wg_t, q_norm[0:1], kv_norm[0:1], wuq_t, wuk_p, wuv_t,
        cos_q, sin_q, cos_k, sin_k, conv_w[0], conv_b[0:1], tm)

    a_t = _attention(q_t, kcat, v_t, g_t)

    yf, yb = _ssd(u_ssd, dt_raw, _pad_row(jnp.tile(a_log[0].reshape(-1), DT_REPL)),
                  _pad_row(jnp.tile(dt_bias[0].reshape(-1), DT_REPL)),
                  jnp.repeat(d_skip[0].astype(F32), SSD_HEAD_DIM, axis=1), nctx)

    wo = w_out_mix[0].astype(BF16)
    x1, u, sg = _mix_out(a_t, yf, yb, sz, x, mod0, mod1, ssd_norm[0:1], wo[:MLA_WIDTH], wo[MLA_WIDTH:],
                         norm_w[1:2], w_in_pool[0].astype(BF16), tm_out)

    return _pool_out(u, sg, x1, mod1, pool_lin[0].astype(BF16), pool_scale[0:1], w_out_pool[0].astype(BF16),
                     final_norm.reshape(1, d), tm_out)
```

```python
import functools
import math

import jax
import jax.numpy as jnp
import numpy as np
from jax import lax
from jax.experimental import pallas as pl
from jax.experimental.pallas import tpu as pltpu

D_MODEL = 1024
GRID_W = 64

MLA_HEADS = 8
MLA_NOPE = 64
MLA_ROPE = 32
MLA_V = 64
MLA_Q_RANK = 384
MLA_KV_RANK = 256
MLA_WIDTH = MLA_HEADS * MLA_V
ROPE_BASE = 10000.0
HEAD_PAD = 128
V_ROWS = 80

SSD_HEADS = 8
SSD_HEAD_DIM = 64
SSD_WIDTH = SSD_HEADS * SSD_HEAD_DIM
SSD_GROUPS = 2
SSD_STATE = 64
SSD_CONV = 3
SSD_CHUNK = 128
SSD_CONV_DIM = SSD_WIDTH + 2 * SSD_GROUPS * SSD_STATE
MIX_SPLITS = (MLA_Q_RANK, MLA_KV_RANK, MLA_ROPE, MLA_WIDTH, SSD_WIDTH, SSD_CONV_DIM, 2 * SSD_HEADS)
SSD_STEP_CHUNKS = 2
DT_COLS = 2 * SSD_HEADS
DT_REPL = 3

POOL_WINDOWS = (2, 4, 8, 16)
POOL_GROUPS = len(POOL_WINDOWS)
POOL_WIDTH = D_MODEL
POOL_GROUP_DIM = POOL_WIDTH // POOL_GROUPS
POOL_HALO = 8

RMS_EPS = 1e-6
LANES = 128
SUBLANES = 8
VMEM_LIMIT = 56 * 1024 * 1024

C_QA = 0
C_KVA = C_QA + MLA_Q_RANK
C_KPE = C_KVA + MLA_KV_RANK
C_KPS = C_KPE + LANES
C_Z = C_KPS + LANES
C_XBC = C_Z + SSD_WIDTH
C_DT = C_XBC + SSD_CONV_DIM
C_END = C_DT + LANES

BF16 = jnp.bfloat16
F32 = jnp.float32
NEG_BIG = -1e30


def _rms(x, w):
    y = x * lax.rsqrt(jnp.mean(x * x, axis=-1, keepdims=True) + RMS_EPS)
    return y * w


def _silu(x):
    return x * (1.0 / (1.0 + jnp.exp(-x)))


def _softplus(x):
    return jnp.maximum(x, 0.0) + jnp.log(1.0 + jnp.exp(-jnp.abs(x)))


def _dot(a, b):
    return jnp.dot(a, b, preferred_element_type=F32)


def _dot_nt(a, b):
    return lax.dot_general(a, b, (((1,), (1,)), ((), ())), preferred_element_type=F32)


def _dot_tn(a, b):
    return lax.dot_general(a, b, (((0,), (0,)), ((), ())), preferred_element_type=F32)


def _params(sem):
    return pltpu.CompilerParams(dimension_semantics=sem, vmem_limit_bytes=VMEM_LIMIT)


def _mod_kernel(c_ref, w_ref, b_ref, o_ref):
    sc = _silu(c_ref[...]).astype(BF16)
    o_ref[0] = _dot(sc, w_ref[0].astype(BF16)) + b_ref[0]


def _modulation(cvec, mod_w, mod_b):
    depth, d, d3 = mod_w.shape
    rows = cvec.shape[0]
    nblk = d3 // d
    return pl.pallas_call(
        _mod_kernel,
        out_shape=jax.ShapeDtypeStruct((depth, rows, d3), F32),
        grid=(depth, nblk),
        in_specs=[
            pl.BlockSpec((rows, d), lambda i, j: (0, 0)),
            pl.BlockSpec((1, d, d), lambda i, j: (i, 0, j)),
            pl.BlockSpec((1, 1, d), lambda i, j: (i, 0, j)),
        ],
        out_specs=pl.BlockSpec((1, rows, d), lambda i, j: (i, 0, j)),
        compiler_params=_params(("arbitrary", "arbitrary")),
        name="modulation",
    )(cvec, mod_w, mod_b.reshape(depth, 1, d3))


def _mix_in_kernel(nct, nt, nbatch, x_ref, ctx_ref, mod_ref, nw_ref, wn_ref, wg_ref, qn_ref, kvn_ref,
                   wuq_ref, wuk_ref, wuv_ref, cosq_ref, sinq_ref, cosk_ref, sink_ref, cw_ref, cb_ref,
                   q_ref, k_ref, v_ref, g_ref, sz_ref, u_ref, dt_ref, xbuf_ref, last_ref):
    b = pl.program_id(0)
    t = pl.program_id(1)

    @pl.when(t == 0)
    def _():
        last_ref[...] = jnp.zeros_like(last_ref)
        xbuf_ref[...] = jnp.zeros_like(xbuf_ref)

    def finish_conv(xbc_new):
        xc = xbuf_ref[(t + 1) % 2]
        xbuf_ref[t % 2] = xbc_new
        tp = t - 1
        tm = xc.shape[0]
        first_seg = jnp.logical_or(tp == 0, tp == nct)
        last_seg = jnp.logical_or(tp == nct - 1, tp == nt - 1)
        prev_row = jnp.where(first_seg, 0.0, last_ref[0:1, :])
        next_row = jnp.where(last_seg, 0.0, xbc_new[0:1, :])
        ridx = lax.broadcasted_iota(jnp.int32, (tm, 1), 0)
        x_m1 = jnp.where(ridx == 0, prev_row, pltpu.roll(xc, 1, axis=0))
        x_p1 = jnp.where(ridx == tm - 1, next_row, pltpu.roll(xc, tm - 1, axis=0))
        cw = cw_ref[...]
        u_ref[0] = _silu(cb_ref[...] + x_m1 * cw[0:1] + xc * cw[1:2] + x_p1 * cw[2:3])
        last_ref[0:1, :] = xc[tm - 1:tm]

    _mix_in_project(nct, nbatch, b, jnp.minimum(t, nt - 1), x_ref, ctx_ref, mod_ref, nw_ref, wn_ref,
                    wg_ref, qn_ref, kvn_ref, wuq_ref, wuk_ref, wuv_ref, cosq_ref, sinq_ref,
                    cosk_ref, sink_ref, q_ref, k_ref, v_ref, g_ref, sz_ref, dt_ref, finish_conv)


def _mix_in_project(nct, nbatch, b, t, x_ref, ctx_ref, mod_ref, nw_ref, wn_ref, wg_ref, qn_ref, kvn_ref,
                    wuq_ref, wuk_ref, wuv_ref, cosq_ref, sinq_ref, cosk_ref, sink_ref,
                    q_ref, k_ref, v_ref, g_ref, sz_ref, dt_ref, finish_conv):
    is_ctx = t < nct
    xin = jnp.where(is_ctx, ctx_ref[0], x_ref[0])
    row = jnp.where(is_ctx, nbatch, b)
    mod = mod_ref[pl.ds(row, 1), :]
    shift = mod[:, 0:D_MODEL]
    scale = mod[:, D_MODEL:2 * D_MODEL]
    h = _rms(xin, nw_ref[...]) * (1.0 + scale) + shift
    hb = h.astype(BF16)
    proj = _dot(hb, wn_ref[...])

    dt_ref[0] = proj[:, C_DT:C_END]
    finish_conv(proj[:, C_XBC:C_DT])

    ckv = _rms(proj[:, C_KVA:C_KPE], kvn_ref[...]).astype(BF16)
    kpad = _dot(ckv, wuk_ref[...])
    kpe = proj[:, C_KPE:C_KPS] * cosk_ref[...] + proj[:, C_KPS:C_Z] * sink_ref[...]
    vt = _dot_nt(wuv_ref[...], ckv)
    tm = vt.shape[1]
    ones_rows = (lax.broadcasted_iota(jnp.int32, (V_ROWS - MLA_V, tm), 0) == 0).astype(BF16)
    for hd in range(MLA_HEADS):
        k_ref[0, hd] = (kpad[:, hd * HEAD_PAD:(hd + 1) * HEAD_PAD] + kpe).astype(BF16)
        v_ref[0, hd, 0, 0:MLA_V, :] = vt[hd * MLA_V:(hd + 1) * MLA_V].astype(BF16)
        v_ref[0, hd, 0, MLA_V:V_ROWS, :] = ones_rows

    @pl.when(jnp.logical_not(is_ctx))
    def _():
        sz_ref[0] = _silu(proj[:, C_Z:C_XBC])
        g_ref[0] = _silu(_dot_nt(wg_ref[...], hb))
        cq = _rms(proj[:, C_QA:C_KVA], qn_ref[...]).astype(BF16)
        qt = _dot_nt(wuq_ref[...], cq)
        sm_scale = (MLA_NOPE + MLA_ROPE) ** -0.5 * math.log2(math.e)
        cos = cosq_ref[...]
        sin = sinq_ref[...]
        half = MLA_ROPE // 2
        for hd in range(MLA_HEADS):
            base = hd * HEAD_PAD
            t1 = qt[base + MLA_NOPE:base + MLA_NOPE + half]
            t2 = qt[base + MLA_NOPE + half:base + MLA_NOPE + MLA_ROPE]
            q_ref[0, hd, 0, 0:MLA_NOPE, :] = (qt[base:base + MLA_NOPE] * sm_scale).astype(BF16)
            q_ref[0, hd, 0, MLA_NOPE:MLA_NOPE + half, :] = ((t1 * cos - t2 * sin) * sm_scale).astype(BF16)
            q_ref[0, hd, 0, MLA_NOPE + half:MLA_NOPE + MLA_ROPE, :] = ((t2 * cos + t1 * sin) * sm_scale).astype(BF16)
            q_ref[0, hd, 0, MLA_NOPE + MLA_ROPE:HEAD_PAD, :] = jnp.zeros((HEAD_PAD - MLA_NOPE - MLA_ROPE, tm), BF16)


def _mix_in(x, ctx, mod0, norm_w, wn, wg_t, q_norm, kv_norm, wuq_t, wuk_p, wuv_t,
            cos_q, sin_q, cos_k, sin_k, conv_w, conv_b, tm):
    bsz, n, d = x.shape
    nctx = ctx.shape[1]
    nct = nctx // tm
    nlt = n // tm
    nt = nct + nlt
    tt = nctx + n
    rows = mod0.shape[0]

    def lat(t):
        return jnp.clip(t - nct, 0, nlt - 1)

    def cur(t):
        return jnp.minimum(t, nt - 1)

    const = lambda b, t: (0, 0)
    out_shape = (
        jax.ShapeDtypeStruct((bsz, MLA_HEADS, n // tm, HEAD_PAD, tm), BF16),
        jax.ShapeDtypeStruct((bsz, MLA_HEADS, tt, HEAD_PAD), BF16),
        jax.ShapeDtypeStruct((bsz, MLA_HEADS, tt // tm, V_ROWS, tm), BF16),
        jax.ShapeDtypeStruct((bsz, MLA_WIDTH, n), F32),
        jax.ShapeDtypeStruct((bsz, n, SSD_WIDTH), F32),
        jax.ShapeDtypeStruct((bsz, tt, SSD_CONV_DIM), F32),
        jax.ShapeDtypeStruct((bsz, tt, LANES), F32),
    )
    out_specs = (
        pl.BlockSpec((1, MLA_HEADS, 1, HEAD_PAD, tm), lambda b, t: (b, 0, lat(t), 0, 0)),
        pl.BlockSpec((1, MLA_HEADS, tm, HEAD_PAD), lambda b, t: (b, 0, cur(t), 0)),
        pl.BlockSpec((1, MLA_HEADS, 1, V_ROWS, tm), lambda b, t: (b, 0, cur(t), 0, 0)),
        pl.BlockSpec((1, MLA_WIDTH, tm), lambda b, t: (b, 0, lat(t))),
        pl.BlockSpec((1, tm, SSD_WIDTH), lambda b, t: (b, lat(t), 0)),
        pl.BlockSpec((1, tm, SSD_CONV_DIM), lambda b, t: (b, jnp.maximum(t - 1, 0), 0)),
        pl.BlockSpec((1, tm, LANES), lambda b, t: (b, cur(t), 0)),
    )
    in_specs = [
        pl.BlockSpec((1, tm, d), lambda b, t: (b, lat(t), 0)),
        pl.BlockSpec((1, tm, d), lambda b, t: (b, jnp.minimum(t, nct - 1), 0)),
        pl.BlockSpec((rows, 3 * d), const),
        pl.BlockSpec((1, d), const),
        pl.BlockSpec(wn.shape, const),
        pl.BlockSpec(wg_t.shape, const),
        pl.BlockSpec((1, MLA_Q_RANK), const),
        pl.BlockSpec((1, MLA_KV_RANK), const),
        pl.BlockSpec(wuq_t.shape, const),
        pl.BlockSpec(wuk_p.shape, const),
        pl.BlockSpec(wuv_t.shape, const),
        pl.BlockSpec((MLA_ROPE // 2, tm), lambda b, t: (0, lat(t))),
        pl.BlockSpec((MLA_ROPE // 2, tm), lambda b, t: (0, lat(t))),
        pl.BlockSpec((tm, LANES), lambda b, t: (cur(t), 0)),
        pl.BlockSpec((tm, LANES), lambda b, t: (cur(t), 0)),
        pl.BlockSpec(conv_w.shape, const),
        pl.BlockSpec(conv_b.shape, const),
    ]
    return pl.pallas_call(
        functools.partial(_mix_in_kernel, nct, nt, bsz),
        out_shape=out_shape,
        grid=(bsz, nt + 1),
        in_specs=in_specs,
        out_specs=out_specs,
        scratch_shapes=[pltpu.VMEM((2, tm, SSD_CONV_DIM), F32), pltpu.VMEM((SUBLANES, SSD_CONV_DIM), F32)],
        compiler_params=_params(("arbitrary", "arbitrary")),
        name="mix_in",
    )(x, ctx, mod0, norm_w, wn, wg_t, q_norm, kv_norm, wuq_t, wuk_p, wuv_t, cos_q, sin_q, cos_k, sin_k,
      conv_w, conv_b)


def _attn_kernel(nkb, sub, kt, ntiles, q_ref, k_ref, v_ref, g_ref, qn_ref, kn_ref, o_ref, sa_ref, sb_ref, ma_ref):
    i = pl.program_id(2)
    first_step = jnp.logical_and(jnp.logical_and(pl.program_id(0) == 0, pl.program_id(1) == 0), i == 0)
    tq = 2 * q_ref.shape[4]
    tps = o_ref.shape[2] // tq
    last_of_head = i == ntiles // tps - 1
    kb = sub * kt

    def q_tile(t):
        return jnp.concatenate([q_ref[0, 0, 2 * t], q_ref[0, 0, 2 * t + 1]], axis=1)

    def keys(j, look_ahead):
        rows = slice(j * kb, (j + 1) * kb)
        if look_ahead:
            return jnp.where(last_of_head, kn_ref[0, 0, rows, :], k_ref[0, 0, rows, :])
        return k_ref[0, 0, rows, :]

    def scores(j, q, s_ref, look_ahead=False):
        s = _dot(keys(j, look_ahead), q)
        s_ref[j] = s
        while s.shape[0] > SUBLANES and s.shape[0] % (2 * SUBLANES) == 0:
            half = s.shape[0] // 2
            s = jnp.maximum(s[:half], s[half:])
        return jnp.max(s, axis=0, keepdims=True)

    def weighted_values(j, s_ref, m, acc):
        p = jnp.exp2(s_ref[j] - m).astype(BF16)
        for r in range(sub):
            acc = acc + _dot(v_ref[0, 0, j * sub + r], p[r * kt:(r + 1) * kt])
        return acc

    def fused(q_next, w_ref, r_ref, m_read, look_ahead=False):
        acc = jnp.zeros((V_ROWS, tq), F32)
        m_next = jnp.full((1, tq), NEG_BIG, F32)
        for j in range(nkb):
            m_next = jnp.maximum(m_next, scores(j, q_next, w_ref, look_ahead))
            acc = weighted_values(j, r_ref, m_read, acc)
        return acc, m_next

    def finish(acc, lo):
        out = acc[0:MLA_V] / acc[MLA_V:MLA_V + 1]
        o_ref[0, :, lo:lo + tq] = (out * g_ref[0, :, lo:lo + tq]).astype(BF16)

    @pl.when(first_step)
    def _():
        q0 = q_tile(0)
        m0 = jnp.full((1, tq), NEG_BIG, F32)
        for j in range(nkb):
            m0 = jnp.maximum(m0, scores(j, q0, sa_ref))
        ma_ref[0:1, :] = m0

    bufs = (sa_ref, sb_ref)
    m_read = ma_ref[0:1, :]
    for k in range(tps):
        if k < tps - 1:
            q_next, ahead = q_tile(tps * i + k + 1), False
        else:
            q_ahead = jnp.concatenate([qn_ref[0, 0, 0], qn_ref[0, 0, 1]], axis=1)
            q_next = jnp.where(last_of_head, q_ahead, q_tile(jnp.minimum(tps * i + tps, ntiles - 1)))
            ahead = True
        acc, m_read = fused(q_next, bufs[(k + 1) % 2], bufs[k % 2], m_read, look_ahead=ahead)
        finish(acc, k * tq)
    ma_ref[0:1, :] = m_read


def _attention(q_t, kcat, v_t, g_t):
    bsz, nh, nqt, _, qt = q_t.shape
    tq = 2 * qt
    n = nqt * qt
    ntiles = n // tq
    tt = kcat.shape[2]
    nvb, kt = v_t.shape[2], v_t.shape[4]
    sub = 1
    nkb = nvb // sub
    tps = 4 if ntiles % 4 == 0 else 2
    assert ntiles % tps == 0

    def nxt(b, h):
        f = jnp.minimum(b * nh + h + 1, bsz * nh - 1)
        return f // nh, f % nh

    return pl.pallas_call(
        functools.partial(_attn_kernel, nkb, sub, kt, ntiles),
        out_shape=jax.ShapeDtypeStruct((bsz, nh * MLA_V, n), BF16),
        grid=(bsz, nh, ntiles // tps),
        in_specs=[
            pl.BlockSpec((1, 1, nqt, HEAD_PAD, qt), lambda b, h, i: (b, h, 0, 0, 0)),
            pl.BlockSpec((1, 1, tt, HEAD_PAD), lambda b, h, i: (b, h, 0, 0)),
            pl.BlockSpec((1, 1, nvb, V_ROWS, kt), lambda b, h, i: (b, h, 0, 0, 0)),
            pl.BlockSpec((1, MLA_V, tps * tq), lambda b, h, i: (b, h, i)),
            pl.BlockSpec((1, 1, 2, HEAD_PAD, qt), lambda b, h, i: (*nxt(b, h), 0, 0, 0)),
            pl.BlockSpec((1, 1, tt, HEAD_PAD), lambda b, h, i: (*nxt(b, h), 0, 0)),
        ],
        out_specs=pl.BlockSpec((1, MLA_V, tps * tq), lambda b, h, i: (b, h, i)),
        scratch_shapes=[pltpu.VMEM((nkb, sub * kt, tq), F32), pltpu.VMEM((nkb, sub * kt, tq), F32),
                        pltpu.VMEM((SUBLANES, tq), F32)],
        compiler_params=_params(("arbitrary", "arbitrary", "arbitrary")),
        name="attention",
    )(q_t, kcat, v_t, g_t, q_t, kcat)


def _ssd_direction(d, u, dtm, acs, skip_ref, eh_ref, ew_ref, state_ref, y_ref, y_row):
    L = SSD_CHUNK
    xs = u[:, 0:SSD_WIDTH]
    gn = SSD_GROUPS * SSD_STATE
    bm = u[:, SSD_WIDTH:SSD_WIDTH + gn].astype(BF16)
    cm = u[:, SSD_WIDTH + gn:SSD_WIDTH + 2 * gn].astype(BF16)

    e_head = eh_ref[d]
    e_wide = ew_ref[d]
    dt_x = _dot(_split3(dtm), e_head)
    acs3 = _split3(acs)
    acs_x = _dot(acs3, e_head)
    acs_t = acs.T
    end = L - 1 if d == 0 else 0
    a_end = acs_x[end:end + 1, :]

    li = lax.broadcasted_iota(jnp.int32, (L, L), 0)
    si = lax.broadcasted_iota(jnp.int32, (L, L), 1)
    mask = (si <= li) if d == 0 else (si >= li)
    lane = lax.broadcasted_iota(jnp.int32, (L, LANES), 1)
    low = lane < SSD_STATE
    zero_b = jnp.zeros((L, LANES), BF16)

    xd = xs * dt_x
    xdb = xd.astype(BF16)
    heads_per_group = SSD_HEADS // SSD_GROUPS
    ys = []
    for g in range(SSD_GROUPS):
        cg = jnp.where(low, cm, zero_b) if g == 0 else jnp.where(low, zero_b, cm)
        cbm = _dot_nt(cg, bm)
        for pr in range(heads_per_group // 2):
            ms = []
            h0 = g * heads_per_group + 2 * pr
            acs_w = _dot(acs3, e_wide[:, h0 * L:(h0 + 2) * L])
            for r in range(2):
                hd = h0 + r
                col = d * SSD_HEADS + hd
                diff = acs_w[:, r * L:(r + 1) * L] - acs_t[col:col + 1, :]
                lmat = jnp.exp(jnp.where(mask, diff, -jnp.inf))
                ms.append((cbm * lmat).astype(BF16))
            pair = (g * heads_per_group) // 2 + pr
            xp = xdb[:, pair * LANES:(pair + 1) * LANES]
            rhs = jnp.concatenate([jnp.where(low, xp, zero_b), jnp.where(low, zero_b, xp)], axis=0)
            ys.append(_dot(jnp.concatenate(ms, axis=1), rhs))
    h_in = state_ref[d]
    y = jnp.concatenate(ys, axis=1) + jnp.exp(acs_x) * _dot(cm, h_in.astype(BF16)) + skip_ref[d:d + 1, :] * xs
    y_ref[0, y_row:y_row + L, :] = y

    xdw = (xd * jnp.exp(a_end - acs_x)).astype(BF16)
    st = _dot_tn(bm, xdw)
    gi = lax.broadcasted_iota(jnp.int32, st.shape, 0) // SSD_STATE
    hi = lax.broadcasted_iota(jnp.int32, st.shape, 1) // (heads_per_group * SSD_HEAD_DIM)
    state_ref[d] = h_in * jnp.exp(a_end) + jnp.where(gi == hi, st, 0.0)


def _split3(x):
    lane = lax.broadcasted_iota(jnp.int32, x.shape, 1)
    r1 = x - x.astype(BF16).astype(F32)
    r2 = r1 - r1.astype(BF16).astype(F32)
    return jnp.where(lane < DT_COLS, x, jnp.where(lane < 2 * DT_COLS, r1, r2)).astype(BF16)


def _ssd_kernel(uf_ref, dtf_ref, ub_ref, dtb_raw_ref, alog_ref, dtbias_ref, skip_ref, tri_ref, eh_ref, ew_ref,
                yf_ref, yb_ref, state_ref):
    s = pl.program_id(1)
    L = SSD_CHUNK
    nsub = SSD_STEP_CHUNKS

    @pl.when(s == 0)
    def _():
        state_ref[...] = jnp.zeros_like(state_ref)

    neg_a = -jnp.exp(alog_ref[...])
    dtm_f = _softplus(dtf_ref[0] + dtbias_ref[...])
    dtm_b = _softplus(dtb_raw_ref[0] + dtbias_ref[...])
    a_f = dtm_f * neg_a
    a_b = dtm_b * neg_a
    rows = [slice(c * L, (c + 1) * L) for c in range(nsub)]
    a_all = jnp.concatenate([a_f[r] for r in rows] + [a_b[r] for r in rows], axis=1)
    a_hi = a_all.astype(BF16)
    r1 = a_all - a_hi.astype(F32)
    a_mid = r1.astype(BF16)
    a_lo = (r1 - a_mid.astype(F32)).astype(BF16)
    tri = tri_ref[...]
    cs = _dot(tri, a_hi) + (_dot(tri, a_mid) + _dot(tri, a_lo))
    uf = uf_ref[0]
    ub = ub_ref[0]
    for c in range(nsub):
        cf, cbk = c, nsub - 1 - c
        acs_f = cs[:, cf * LANES:(cf + 1) * LANES]
        cs_b = cs[:, (nsub + cbk) * LANES:(nsub + cbk + 1) * LANES]
        acs_b = cs_b[L - 1:L, :] - cs_b + a_b[rows[cbk]]
        _ssd_direction(0, uf[rows[cf]], dtm_f[rows[cf]], acs_f, skip_ref, eh_ref, ew_ref, state_ref, yf_ref, cf * L)
        _ssd_direction(1, ub[rows[cbk]], dtm_b[rows[cbk]], acs_b, skip_ref, eh_ref, ew_ref, state_ref, yb_ref,
                       cbk * L)


def _ssd_constants():
    L = SSD_CHUNK
    tri = np.tril(np.ones((L, L), np.float32))
    e_head = np.zeros((2, LANES, SSD_WIDTH), np.float32)
    e_wide = np.zeros((2, LANES, SSD_HEADS * L), np.float32)
    for d in range(2):
        for rep in range(DT_REPL):
            for h in range(SSD_HEADS):
                r = rep * DT_COLS + d * SSD_HEADS + h
                e_head[d, r, h * SSD_HEAD_DIM:(h + 1) * SSD_HEAD_DIM] = 1.0
                e_wide[d, r, h * L:(h + 1) * L] = 1.0
    return jnp.asarray(tri, BF16), jnp.asarray(e_head, BF16), jnp.asarray(e_wide, BF16)


def _ssd(u, dt_raw, alog_row, dtbias_row, skip_x, nctx):
    bsz, tt, cd = u.shape
    tri, e_head, e_wide = _ssd_constants()
    L = SSD_STEP_CHUNKS * SSD_CHUNK
    assert nctx % L == 0 and tt % L == 0
    nchunks = tt // L
    ncc = nctx // L
    n = tt - nctx

    def cf(s):
        return s

    def cb(s):
        return jnp.where(s < ncc, ncc - 1 - s, nchunks - 1 - (s - ncc))

    def specs(cfn):
        return [
            pl.BlockSpec((1, L, cd), lambda b, s: (b, cfn(s), 0)),
            pl.BlockSpec((1, L, LANES), lambda b, s: (b, cfn(s), 0)),
        ]

    const = lambda b, s: (0, 0)
    in_specs = specs(cf) + specs(cb) + [
        pl.BlockSpec((1, LANES), const),
        pl.BlockSpec((1, LANES), const),
        pl.BlockSpec(skip_x.shape, const),
        pl.BlockSpec(tri.shape, const),
        pl.BlockSpec(e_head.shape, lambda b, s: (0, 0, 0)),
        pl.BlockSpec(e_wide.shape, lambda b, s: (0, 0, 0)),
    ]
    out_specs = (
        pl.BlockSpec((1, L, SSD_WIDTH), lambda b, s: (b, jnp.maximum(cf(s) - ncc, 0), 0)),
        pl.BlockSpec((1, L, SSD_WIDTH), lambda b, s: (b, nchunks - 1 - ncc - jnp.maximum(s - ncc, 0), 0)),
    )
    y_shape = jax.ShapeDtypeStruct((bsz, n, SSD_WIDTH), F32)
    return pl.pallas_call(
        _ssd_kernel,
        out_shape=(y_shape, y_shape),
        grid=(bsz, nchunks),
        in_specs=in_specs,
        out_specs=out_specs,
        scratch_shapes=[pltpu.VMEM((2, SSD_GROUPS * SSD_STATE, SSD_WIDTH), F32)],
        compiler_params=_params(("arbitrary", "arbitrary")),
        name="ssd",
    )(u, dt_raw, u, dt_raw, alog_row, dtbias_row, skip_x, tri, e_head, e_wide)


def _mix_out_kernel(nbatch, a_ref, yf_ref, yb_ref, sz_ref, x_ref, mod0_ref, mod1_ref, sn_ref, woa_ref, wos_ref,
                    nw1_ref, wp_ref, x1_ref, u_ref, sg_ref):
    b = pl.program_id(0)
    y = (yf_ref[0] + yb_ref[0]) * sz_ref[0]
    ssd = _rms(y, sn_ref[...]).astype(BF16)
    o = _dot_tn(a_ref[0], woa_ref[...]) + _dot(ssd, wos_ref[...])
    gate0 = mod0_ref[pl.ds(b, 1), 2 * D_MODEL:3 * D_MODEL]
    x1 = x_ref[0] + gate0 * o
    x1_ref[0] = x1
    mod1 = mod1_ref[pl.ds(b, 1), :]
    h1 = _rms(x1, nw1_ref[...]) * (1.0 + mod1[:, D_MODEL:2 * D_MODEL]) + mod1[:, 0:D_MODEL]
    ug = _dot(h1.astype(BF16), wp_ref[...])
    u_ref[0] = ug[:, 0:POOL_WIDTH]
    sg_ref[0] = _silu(ug[:, POOL_WIDTH:2 * POOL_WIDTH])


def _mix_out(a_t, yf, yb, sz, x, mod0, mod1, ssd_norm, wo_a, wo_s, norm_w1, w_pool, tm):
    bsz, n, d = x.shape
    rows = mod0.shape[0]
    const = lambda b, t: (0, 0)
    tile = lambda w: pl.BlockSpec((1, tm, w), lambda b, t: (b, t, 0))
    return pl.pallas_call(
        functools.partial(_mix_out_kernel, bsz),
        out_shape=(jax.ShapeDtypeStruct((bsz, n, d), F32),
                   jax.ShapeDtypeStruct((bsz, n, POOL_WIDTH), F32),
                   jax.ShapeDtypeStruct((bsz, n, POOL_WIDTH), F32)),
        grid=(bsz, n // tm),
        in_specs=[
            pl.BlockSpec((1, MLA_WIDTH, tm), lambda b, t: (b, 0, t)),
            tile(SSD_WIDTH), tile(SSD_WIDTH), tile(SSD_WIDTH), tile(d),
            pl.BlockSpec((rows, 3 * d), const),
            pl.BlockSpec((rows, 3 * d), const),
            pl.BlockSpec((1, SSD_WIDTH), const),
            pl.BlockSpec(wo_a.shape, const),
            pl.BlockSpec(wo_s.shape, const),
            pl.BlockSpec((1, d), const),
            pl.BlockSpec(w_pool.shape, const),
        ],
        out_specs=(tile(d), tile(POOL_WIDTH), tile(POOL_WIDTH)),
        compiler_params=_params(("arbitrary", "arbitrary")),
        name="mix_out",
    )(a_t, yf, yb, sz, x, mod0, mod1, ssd_norm, wo_a, wo_s, norm_w1, w_pool)


def _pool_kernel(n, u_ref, up_ref, un_ref, sg_ref, x1_ref, mod1_ref, lin_ref, ps_ref, wo_ref, fn_ref,
                 o_ref, ext_ref):
    b = pl.program_id(0)
    t = pl.program_id(1)
    nt = pl.num_programs(1)
    tm = u_ref.shape[1]
    H = POOL_HALO
    ext_ref[0:H, :] = jnp.where(t == 0, 0.0, up_ref[0])
    ext_ref[H:H + tm, :] = u_ref[0]
    ext_ref[H + tm:H + tm + H, :] = jnp.where(t == nt - 1, 0.0, un_ref[0])

    tok = t * tm + lax.broadcasted_iota(jnp.int32, (tm, 1), 0)
    rows = tm + 2 * H
    ys = []
    for g, w in enumerate(POOL_WINDOWS):
        lanes = slice(g * POOL_GROUP_DIM, (g + 1) * POOL_GROUP_DIM)
        e = ext_ref[:, lanes]
        win = e + pltpu.roll(e, 1, axis=0)
        width = 2
        while width < w:
            win = pltpu.roll(win, width // 2, axis=0) + pltpu.roll(win, rows - width // 2, axis=0)
            width *= 2
        acc = win[H:H + tm]
        lo = jnp.maximum(tok - w // 2, 0)
        hi = jnp.minimum(tok + (w - w // 2 - 1), n - 1)
        cnt = (hi - lo + 1).astype(F32)
        m = acc / cnt - u_ref[0, :, lanes]
        ys.append(_dot(m.astype(BF16), lin_ref[g]))
    y = jnp.concatenate(ys, axis=-1) * ps_ref[...] * sg_ref[0]
    o = _dot(y.astype(BF16), wo_ref[...])
    gate1 = mod1_ref[pl.ds(b, 1), 2 * D_MODEL:3 * D_MODEL]
    x2 = x1_ref[0] + gate1 * o
    o_ref[0] = _rms(x2, fn_ref[...])


def _pool_out(u, sg, x1, mod1, pool_lin, pool_scale, wo_pool, final_norm, tm):
    bsz, n, d = x1.shape
    rows = mod1.shape[0]
    r8 = tm // SUBLANES
    nrow8 = n // SUBLANES
    const = lambda b, t: (0, 0)
    tile = lambda w: pl.BlockSpec((1, tm, w), lambda b, t: (b, t, 0))
    return pl.pallas_call(
        functools.partial(_pool_kernel, n),
        out_shape=jax.ShapeDtypeStruct((bsz, n, d), F32),
        grid=(bsz, n // tm),
        in_specs=[
            tile(POOL_WIDTH),
            pl.BlockSpec((1, POOL_HALO, POOL_WIDTH), lambda b, t: (b, jnp.maximum(t * r8 - 1, 0), 0)),
            pl.BlockSpec((1, POOL_HALO, POOL_WIDTH), lambda b, t: (b, jnp.minimum((t + 1) * r8, nrow8 - 1), 0)),
            tile(POOL_WIDTH), tile(d),
            pl.BlockSpec((rows, 3 * d), const),
            pl.BlockSpec(pool_lin.shape, lambda b, t: (0, 0, 0)),
            pl.BlockSpec((1, POOL_WIDTH), const),
            pl.BlockSpec(wo_pool.shape, const),
            pl.BlockSpec((1, d), const),
        ],
        out_specs=tile(d),
        scratch_shapes=[pltpu.VMEM((tm + 2 * POOL_HALO, POOL_WIDTH), F32)],
        compiler_params=_params(("arbitrary", "arbitrary")),
        name="pool_out",
    )(u, u, u, sg, x1, mod1, pool_lin, pool_scale, wo_pool, final_norm)


def _rope_tables(n, nctx):
    rows = n // GRID_W
    row = jnp.repeat(jnp.arange(rows, dtype=F32), GRID_W)
    col = jnp.tile(jnp.arange(GRID_W, dtype=F32), rows)
    axis_dim = MLA_ROPE // 2
    inv_freq = 1.0 / (ROPE_BASE ** (jnp.arange(0, axis_dim, 2, dtype=F32) / axis_dim))
    ang = jnp.concatenate([row[:, None] * inv_freq, col[:, None] * inv_freq], axis=-1)
    cos, sin = jnp.cos(ang), jnp.sin(ang)
    half = MLA_ROPE // 2
    cos_all = jnp.concatenate([jnp.ones((nctx, half), F32), cos], axis=0)
    sin_all = jnp.concatenate([jnp.zeros((nctx, half), F32), sin], axis=0)
    left = jnp.zeros((nctx + n, MLA_NOPE), F32)
    right = jnp.zeros((nctx + n, LANES - MLA_NOPE - MLA_ROPE), F32)
    cos_k = jnp.concatenate([left, cos_all, cos_all, right], axis=1)
    sin_k = jnp.concatenate([left, -sin_all, sin_all, right], axis=1)
    return cos.T, sin.T, cos_k, sin_k


def _pad_row(v):
    v = v.reshape(1, -1).astype(F32)
    return jnp.pad(v, ((0, 0), (0, LANES - v.shape[1])))


def kernel(x, c, ctx, c_ctx, mod_w, mod_b, norm_w, w_in_mix, q_norm, w_uq, kv_norm, w_ukv, conv_w, conv_b,
           a_log, dt_bias, d_skip, ssd_norm, w_out_mix, w_in_pool, pool_lin, pool_scale, w_out_pool, final_norm):
    bsz, n, d = x.shape
    nctx = ctx.shape[1]
    tm = 256
    tm_out = 512
    tm_pool = 1024
    assert d == D_MODEL and n % tm_pool == 0 and nctx % tm == 0 and n % GRID_W == 0
    assert mod_w.shape[0] == 2, "two layers: one mixing layer followed by one pooling layer"

    rows = -(-(bsz + 1) // SUBLANES) * SUBLANES
    cvec = jnp.zeros((rows, d), F32).at[:bsz].set(c).at[bsz].set(c_ctx)
    mods = _modulation(cvec, mod_w, mod_b)
    mod0, mod1 = mods[0], mods[1]

    w = w_in_mix[0]
    offs = np.cumsum((0,) + MIX_SPLITS)
    w_qa, w_kva, w_kpe, w_ga, w_z, w_xbc, w_dt = [w[:, offs[i]:offs[i + 1]] for i in range(len(MIX_SPLITS))]
    half = MLA_ROPE // 2
    kpe_pad = jnp.zeros((d, LANES), F32).at[:, MLA_NOPE:MLA_NOPE + MLA_ROPE].set(w_kpe)
    kpe_swap = jnp.concatenate([w_kpe[:, half:], w_kpe[:, :half]], axis=1)
    kps_pad = jnp.zeros((d, LANES), F32).at[:, MLA_NOPE:MLA_NOPE + MLA_ROPE].set(kpe_swap)
    dt_pad = jnp.pad(jnp.tile(w_dt, (1, DT_REPL)), ((0, 0), (0, LANES - DT_REPL * DT_COLS)))
    wn = jnp.concatenate([w_qa, w_kva, kpe_pad, kps_pad, w_z, w_xbc, dt_pad], axis=1).astype(BF16)
    wg_t = w_ga.T.astype(BF16)
    wuq = w_uq[0].reshape(MLA_Q_RANK, MLA_HEADS, MLA_NOPE + MLA_ROPE)
    wuq = jnp.pad(wuq, ((0, 0), (0, 0), (0, HEAD_PAD - MLA_NOPE - MLA_ROPE)))
    wuq_t = wuq.reshape(MLA_Q_RANK, MLA_HEADS * HEAD_PAD).T.astype(BF16)
    wukv = w_ukv[0].reshape(MLA_KV_RANK, MLA_HEADS, MLA_NOPE + MLA_V)
    wuk_p = jnp.pad(wukv[:, :, :MLA_NOPE], ((0, 0), (0, 0), (0, HEAD_PAD - MLA_NOPE)))
    wuk_p = wuk_p.reshape(MLA_KV_RANK, MLA_HEADS * HEAD_PAD).astype(BF16)
    wuv_t = wukv[:, :, MLA_NOPE:].reshape(MLA_KV_RANK, MLA_HEADS * MLA_V).T.astype(BF16)
    cos_q, sin_q, cos_k, sin_k = _rope_tables(n, nctx)

    q_t, kcat, v_t, g_t, sz, u_ssd, dt_raw = _mix_in(
        x, ctx, mod0, norm_w[0:1], wn, wg_t, q_norm[0:1], kv_norm[0:1], wuq_t, wuk_p, wuv_t,
        cos_q, sin_q, cos_k, sin_k, conv_w[0], conv_b[0:1], tm)

    a_t = _attention(q_t, kcat, v_t, g_t)

    yf, yb = _ssd(u_ssd, dt_raw, _pad_row(jnp.tile(a_log[0].reshape(-1), DT_REPL)),
                  _pad_row(jnp.tile(dt_bias[0].reshape(-1), DT_REPL)),
                  jnp.repeat(d_skip[0].astype(F32), SSD_HEAD_DIM, axis=1), nctx)

    wo = w_out_mix[0].astype(BF16)
    x1, u, sg = _mix_out(a_t, yf, yb, sz, x, mod0, mod1, ssd_norm[0:1], wo[:MLA_WIDTH], wo[MLA_WIDTH:],
                         norm_w[1:2], w_in_pool[0].astype(BF16), tm_out)

    return _pool_out(u, sg, x1, mod1, pool_lin[0].astype(BF16), pool_scale[0:1], w_out_pool[0].astype(BF16),
                     final_norm.reshape(1, d), tm_pool)
```
